```python
import math
import jax, jax.numpy as jnp
from jax import lax
import numpy as np

D_MODEL = 1024
BATCH = 4
SEQ = 8192
DEPTH = 1

DA_HEADS = 8
DA_HEAD_DIM = 64
DA_V_DIM = 2 * DA_HEAD_DIM
DA_QK_WIDTH = DA_HEADS * 2 * DA_HEAD_DIM
DA_V_WIDTH = DA_HEADS * DA_V_DIM
ROPE_THETA = 500000.0
ROT_DIM = DA_HEAD_DIM // 4
Q_BLOCK = 128
SG_GROUPS = 8
SG_CHUNK = 128
SG_WIDTH = 1024
SG_GROUP_DIM = SG_WIDTH // SG_GROUPS
MEM_LEN = 256
XA_HEADS = 4
XA_HEAD_DIM = 256
XA_WIDTH = XA_HEADS * XA_HEAD_DIM
N_BRANCHES = 3
D_FF = int(math.ceil(8 * D_MODEL / 3 / 256)) * 256
ALPHA = (2 * DEPTH) ** 0.25
BETA = (8 * DEPTH) ** -0.25
LN_EPS = 1e-5
RMS_EPS = 1e-5

SPLIT_SIZES = (DA_QK_WIDTH, DA_QK_WIDTH, DA_V_WIDTH, SG_WIDTH, SG_WIDTH, XA_WIDTH, N_BRANCHES * D_MODEL)
SPLIT_POINTS = [int(i) for i in np.cumsum(SPLIT_SIZES)[:-1]]
IN_WIDTH = int(sum(SPLIT_SIZES))

kernel_name = "hybrid_diffattn_sgu_memxattn_deepnorm"


def layer_norm(x, g, b):
    xf = x.astype(jnp.float32)
    mu = jnp.mean(xf, axis=-1, keepdims=True)
    var = jnp.mean(jnp.square(xf - mu), axis=-1, keepdims=True)
    y = (xf - mu) * lax.rsqrt(var + LN_EPS) * g.astype(jnp.float32) + b.astype(jnp.float32)
    return y.astype(x.dtype)


def rms_norm(x, g):
    xf = x.astype(jnp.float32)
    y = xf * lax.rsqrt(jnp.mean(jnp.square(xf), axis=-1, keepdims=True) + RMS_EPS) * g.astype(jnp.float32)
    return y.astype(x.dtype)


def apply_partial_rope(t, cos, sin):
    half = ROT_DIM // 2
    t1 = t[..., :half].astype(jnp.float32)
    t2 = t[..., half:ROT_DIM].astype(jnp.float32)
    rot = jnp.concatenate([t1 * cos - t2 * sin, t2 * cos + t1 * sin], axis=-1).astype(t.dtype)
    return jnp.concatenate([rot, t[..., ROT_DIM:]], axis=-1)


def diff_attention(q, k, v, lam):
    B, H, _, S, dh = q.shape
    dv = v.shape[-1]
    nblk = S // Q_BLOCK
    scale = dh ** -0.5
    qb = q.reshape(B, H, 2, nblk, Q_BLOCK, dh).transpose(3, 0, 1, 2, 4, 5)
    kpos = jnp.arange(S)

    def one_block(args):
        qblk, start = args
        s = jnp.einsum('bhmqd,bhmkd->bhmqk', qblk, k).astype(jnp.float32) * scale
        qpos = start + jnp.arange(Q_BLOCK)
        causal = kpos[None, :] <= qpos[:, None]
        p = jax.nn.softmax(jnp.where(causal, s, -jnp.inf), axis=-1)
        w = p[:, :, 0] - lam * p[:, :, 1]
        return jnp.einsum('bhqk,bhkd->bhqd', w.astype(v.dtype), v)

    starts = jnp.arange(nblk) * Q_BLOCK
    out = lax.map(one_block, (qb, starts))
    return out.transpose(1, 2, 0, 3, 4).reshape(B, H, S, dv)


def spatial_gating(u, v, w_s, b_s, norm_g, norm_b):
    B, S, _ = u.shape
    nc = S // SG_CHUNK
    v = layer_norm(v, norm_g, norm_b)
    vc = v.reshape(B, nc, SG_CHUNK, SG_GROUPS, SG_GROUP_DIM)
    causal = jnp.tril(jnp.ones((SG_CHUNK, SG_CHUNK), dtype=bool))
    ws = jnp.where(causal[None], w_s, jnp.zeros_like(w_s))
    s = jnp.einsum('gts,bcsgd->bctgd', ws, vc) + b_s.T[None, None, :, :, None]
    return u * s.reshape(B, S, SG_WIDTH)


def memory_cross_attention(xq, mem_kv):
    B, S, _ = xq.shape
    M = mem_kv.shape[1]
    q = xq.reshape(B, S, XA_HEADS, XA_HEAD_DIM)
    k, v = jnp.split(mem_kv, 2, axis=-1)
    k = k.reshape(B, M, XA_HEADS, XA_HEAD_DIM)
    v = v.reshape(B, M, XA_HEADS, XA_HEAD_DIM)
    s = jnp.einsum('bshd,bmhd->bhsm', q, k).astype(jnp.float32) * (XA_HEAD_DIM ** -0.5)
    p = jax.nn.softmax(s, axis=-1).astype(v.dtype)
    return jnp.einsum('bhsm,bmhd->bshd', p, v).reshape(B, S, XA_WIDTH)


def setup_inputs(seed: int = 0) -> dict:
    key = jax.random.key(seed)
    ks = jax.random.split(key, 32)
    f32 = jnp.float32
    L, D = DEPTH, D_MODEL

    def nrm(k, shape, scale):
        return jax.random.normal(k, shape, f32) * scale

    x = nrm(ks[0], (BATCH, SEQ, D), 1.0)
    mem = nrm(ks[1], (BATCH, MEM_LEN, D), 1.0)
    offsets = jax.random.randint(ks[2], (BATCH, 1), 0, 1024, dtype=jnp.int32)
    positions = (jnp.arange(SEQ, dtype=jnp.int32)[None, :] + offsets).astype(jnp.int32)
    return {
        "x": x,
        "mem": mem,
        "positions": positions,
        "w_in": nrm(ks[3], (L, D, IN_WIDTH), D ** -0.5),
        "lambda_q1": nrm(ks[4], (L, DA_HEAD_DIM), 0.1),
        "lambda_k1": nrm(ks[5], (L, DA_HEAD_DIM), 0.1),
        "lambda_q2": nrm(ks[6], (L, DA_HEAD_DIM), 0.1),
        "lambda_k2": nrm(ks[7], (L, DA_HEAD_DIM), 0.1),
        "da_subln_g": 1.0 + nrm(ks[8], (L, DA_V_DIM), 0.02),
        "sg_norm_g": 1.0 + nrm(ks[9], (L, SG_WIDTH), 0.02),
        "sg_norm_b": nrm(ks[10], (L, SG_WIDTH), 0.02),
        "sg_w_s": nrm(ks[11], (L, SG_GROUPS, SG_CHUNK, SG_CHUNK), SG_CHUNK ** -0.5),
        "sg_b_s": 1.0 + nrm(ks[12], (L, SG_GROUPS, SG_CHUNK), 0.1),
        "w_mem_kv": nrm(ks[13], (L, D, 2 * XA_WIDTH), D ** -0.5),
        "w_br_attn": nrm(ks[14], (L, DA_V_WIDTH, D), DA_V_WIDTH ** -0.5),
        "w_br_sg": nrm(ks[15], (L, SG_WIDTH, D), SG_WIDTH ** -0.5),
        "w_br_mem": nrm(ks[16], (L, XA_WIDTH, D), XA_WIDTH ** -0.5),
        "w_out": nrm(ks[17], (L, D, D), BETA * D ** -0.5),
        "ln1_g": 1.0 + nrm(ks[18], (L, D), 0.02),
        "ln1_b": nrm(ks[19], (L, D), 0.02),
        "w_ffn_in": nrm(ks[20], (L, D, 2 * D_FF), D ** -0.5),
        "w_ffn_out": nrm(ks[21], (L, D_FF, D), BETA * D_FF ** -0.5),
        "ln2_g": 1.0 + nrm(ks[22], (L, D), 0.02),
        "ln2_b": nrm(ks[23], (L, D), 0.02),
    }


def reference(x, mem, positions, w_in, lambda_q1, lambda_k1, lambda_q2, lambda_k2, da_subln_g,
              sg_norm_g, sg_norm_b, sg_w_s, sg_b_s, w_mem_kv, w_br_attn, w_br_sg, w_br_mem, w_out,
              ln1_g, ln1_b, w_ffn_in, w_ffn_out, ln2_g, ln2_b):
    B, S, D = x.shape
    f32 = jnp.float32
    half = ROT_DIM // 2
    inv_freq = ROPE_THETA ** (-jnp.arange(half, dtype=f32) * 2.0 / ROT_DIM)
    ang = positions.astype(f32)[..., None] * inv_freq
    cos = jnp.cos(ang)[:, :, None, None, :]
    sin = jnp.sin(ang)[:, :, None, None, :]

    for l in range(DEPTH):
        lambda_init = 0.8 - 0.6 * math.exp(-0.3 * l)
        z = x @ w_in[l]
        q, k, v, su, sv, xq, gl = jnp.split(z, SPLIT_POINTS, axis=-1)

        q = apply_partial_rope(q.reshape(B, S, DA_HEADS, 2, DA_HEAD_DIM), cos, sin)
        k = apply_partial_rope(k.reshape(B, S, DA_HEADS, 2, DA_HEAD_DIM), cos, sin)
        q = q.transpose(0, 2, 3, 1, 4)
        k = k.transpose(0, 2, 3, 1, 4)
        va = v.reshape(B, S, DA_HEADS, DA_V_DIM).transpose(0, 2, 1, 3)
        lam = (jnp.exp(jnp.sum(lambda_q1[l].astype(f32) * lambda_k1[l].astype(f32)))
               - jnp.exp(jnp.sum(lambda_q2[l].astype(f32) * lambda_k2[l].astype(f32)))
               + lambda_init)
        o_da = diff_attention(q, k, va, lam)
        o_da = rms_norm(o_da, da_subln_g[l]) * (1.0 - lambda_init)
        o_da = o_da.transpose(0, 2, 1, 3).reshape(B, S, DA_V_WIDTH)

        o_sg = spatial_gating(jax.nn.gelu(su), jax.nn.gelu(sv), sg_w_s[l], sg_b_s[l],
                              sg_norm_g[l], sg_norm_b[l])

        o_xa = memory_cross_attention(xq, mem @ w_mem_kv[l])

        g = jax.nn.sigmoid(gl).reshape(B, S, N_BRANCHES, D)
        merged = (g[:, :, 0] * (o_da @ w_br_attn[l])
                  + g[:, :, 1] * (o_sg @ w_br_sg[l])
                  + g[:, :, 2] * (o_xa @ w_br_mem[l]))
        x = layer_norm(ALPHA * x + merged @ w_out[l], ln1_g[l], ln1_b[l])

        a, b = jnp.split(x @ w_ffn_in[l], 2, axis=-1)
        x = layer_norm(ALPHA * x + (jax.nn.silu(a) * b) @ w_ffn_out[l], ln2_g[l], ln2_b[l])
    return x
```

```python
import functools
import math

import jax
import jax.numpy as jnp
from jax import lax
from jax.experimental import pallas as pl
from jax.experimental.pallas import tpu as pltpu

D_MODEL = 1024
DA_HEADS = 8
DA_HEAD_DIM = 64
DA_V_DIM = 2 * DA_HEAD_DIM
ROPE_THETA = 500000.0
ROT_DIM = DA_HEAD_DIM // 4
ROT_HALF = ROT_DIM // 2
SG_GROUPS = 8
SG_CHUNK = 128
SG_GROUP_DIM = 128
XA_HEADS = 4
XA_HEAD_DIM = 256
N_SECTIONS = 9
DEPTH = 1
ALPHA = (2 * DEPTH) ** 0.25
LN_EPS = 1e-5
RMS_EPS = 1e-5
LANES = 128

VMEM_LIMIT = 56 * 1024 * 1024

F32 = jnp.float32
BF16 = jnp.bfloat16


def _dot(a, b):
    return jnp.dot(a, b, preferred_element_type=F32)


def _dot_nt(a, b):
    return lax.dot_general(a, b, (((1,), (1,)), ((), ())), preferred_element_type=F32)


def _layer_norm(x, g, b):
    mu = jnp.mean(x, axis=-1, keepdims=True)
    xc = x - mu
    var = jnp.mean(xc * xc, axis=-1, keepdims=True)
    return xc * lax.rsqrt(var + LN_EPS) * g + b


def _proj_in_kernel(x_ref, w_ref, pos_ref, invf_ref, sgg_ref, sgb_ref, o_ref, xb_ref):
    j = pl.program_id(1)

    @pl.when(j == 0)
    def _():
        xb_ref[...] = x_ref[...].astype(BF16)

    acc = _dot(xb_ref[...], w_ref[...])

    def rope(scale):
        ang = pos_ref[...].astype(F32) * invf_ref[...]
        cos = jnp.cos(ang)
        sin = jnp.sin(ang)
        d = lax.broadcasted_iota(jnp.int32, (1, LANES), 1) % DA_HEAD_DIM
        sin_lo = jnp.where(d < ROT_HALF, -sin, 0.0)
        sin_hi = jnp.where((d >= ROT_HALF) & (d < ROT_DIM), sin, 0.0)
        for h in range(DA_HEADS):
            t = acc[:, h * LANES:(h + 1) * LANES]
            t_up = pltpu.roll(t, LANES - ROT_HALF, 1)
            t_dn = pltpu.roll(t, ROT_HALF, 1)
            r = t * cos + t_up * sin_lo + t_dn * sin_hi
            o_ref[:, h * LANES:(h + 1) * LANES] = (r * scale).astype(BF16)

    @pl.when(j == 0)
    def _():
        rope(DA_HEAD_DIM ** -0.5)

    @pl.when(j == 1)
    def _():
        rope(1.0)

    @pl.when((j == 2) | (j == 5))
    def _():
        o_ref[...] = acc.astype(BF16)

    @pl.when(j == 3)
    def _():
        o_ref[...] = jax.nn.gelu(acc).astype(BF16)

    @pl.when(j == 4)
    def _():
        o_ref[...] = _layer_norm(jax.nn.gelu(acc), sgg_ref[...], sgb_ref[...]).astype(BF16)

    @pl.when(j >= 6)
    def _():
        o_ref[...] = (1.0 / (1.0 + jnp.exp(-acc))).astype(BF16)


def _proj_in(x2, w_in_b, pos2, invf, sgg, sgb, tm):
    n = x2.shape[0]
    return pl.pallas_call(
        _proj_in_kernel,
        grid=(n // tm, N_SECTIONS),
        in_specs=[
            pl.BlockSpec((tm, D_MODEL), lambda i, j: (i, 0)),
            pl.BlockSpec((D_MODEL, D_MODEL), lambda i, j: (0, j)),
            pl.BlockSpec((tm, 1), lambda i, j: (i, 0)),
            pl.BlockSpec((1, LANES), lambda i, j: (0, 0)),
            pl.BlockSpec((1, D_MODEL), lambda i, j: (0, 0)),
            pl.BlockSpec((1, D_MODEL), lambda i, j: (0, 0)),
        ],
        out_specs=pl.BlockSpec((tm, D_MODEL), lambda i, j: (i, j)),
        out_shape=jax.ShapeDtypeStruct((n, N_SECTIONS * D_MODEL), BF16),
        scratch_shapes=[pltpu.VMEM((tm, D_MODEL), BF16)],
        compiler_params=pltpu.CompilerParams(
            dimension_semantics=("arbitrary", "arbitrary"), vmem_limit_bytes=VMEM_LIMIT),
        name="proj_in",
    )(x2, w_in_b, pos2, invf, sgg, sgb)


def _diff_attn_kernel(q_ref, k_ref, v_ref, lq1_ref, lk1_ref, lq2_ref, lk2_ref, g_ref, o_ref,
                      m_ref, l_ref, acc_ref, *, tq, tk, lambda_init):
    i = pl.program_id(2)
    q = q_ref[0]
    lane = lax.broadcasted_iota(jnp.int32, (1, LANES), 1)
    qz = (jnp.where(lane < DA_HEAD_DIM, q, jnp.zeros_like(q)),
          jnp.where(lane >= DA_HEAD_DIM, q, jnp.zeros_like(q)))

    m_ref[...] = jnp.full(m_ref.shape, -jnp.inf, F32)
    l_ref[...] = jnp.zeros(l_ref.shape, F32)
    acc_ref[...] = jnp.zeros(acc_ref.shape, F32)

    def block(jb, masked):
        start = pl.multiple_of(jb * tk, tk)
        kb = k_ref[0, pl.ds(start, tk), :]
        vb = v_ref[0, pl.ds(start, tk), :]
        for mp in range(2):
            s = _dot_nt(qz[mp], kb)
            if masked:
                row = lax.broadcasted_iota(jnp.int32, (tq, tk), 0)
                col = lax.broadcasted_iota(jnp.int32, (tq, tk), 1)
                s = jnp.where(col <= row, s, -jnp.inf)
            m_old = m_ref[mp]
            m_new = jnp.maximum(m_old, jnp.max(s, axis=-1, keepdims=True))
            a = jnp.exp(m_old - m_new)
            p = jnp.exp(s - m_new)
            l_ref[mp] = a * l_ref[mp] + jnp.sum(p, axis=-1, keepdims=True)
            acc_ref[mp] = a * acc_ref[mp] + _dot(p.astype(BF16), vb)
            m_ref[mp] = m_new

    def body(jb, carry):
        block(jb, False)
        return carry

    lax.fori_loop(0, i, body, 0)
    block(i, True)

    lam = (jnp.exp(jnp.sum(lq1_ref[...] * lk1_ref[...], axis=-1, keepdims=True))
           - jnp.exp(jnp.sum(lq2_ref[...] * lk2_ref[...], axis=-1, keepdims=True))
           + lambda_init)
    o = acc_ref[0] / l_ref[0] - lam * (acc_ref[1] / l_ref[1])
    ms = jnp.mean(o * o, axis=-1, keepdims=True)
    o = o * lax.rsqrt(ms + RMS_EPS) * g_ref[...] * (1.0 - lambda_init)
    o_ref[0] = o.astype(BF16)


def _diff_attn(z3, lq1, lk1, lq2, lk2, subln_g, tq, lambda_init):
    bsz, seq, _ = z3.shape
    tk = tq
    kern = functools.partial(_diff_attn_kernel, tq=tq, tk=tk, lambda_init=lambda_init)
    vec = lambda w: pl.BlockSpec((1, w), lambda b, h, i: (0, 0))
    return pl.pallas_call(
        kern,
        grid=(bsz, DA_HEADS, seq // tq),
        in_specs=[
            pl.BlockSpec((1, tq, LANES), lambda b, h, i: (b, i, h)),
            pl.BlockSpec((1, seq, LANES), lambda b, h, i: (b, 0, DA_HEADS + h)),
            pl.BlockSpec((1, seq, LANES), lambda b, h, i: (b, 0, 2 * DA_HEADS + h)),
            vec(DA_HEAD_DIM), vec(DA_HEAD_DIM), vec(DA_HEAD_DIM), vec(DA_HEAD_DIM),
            vec(DA_V_DIM),
        ],
        out_specs=pl.BlockSpec((1, tq, LANES), lambda b, h, i: (b, i, h)),
        out_shape=jax.ShapeDtypeStruct((bsz, seq, DA_HEADS * DA_V_DIM), BF16),
        scratch_shapes=[
            pltpu.VMEM((2, tq, 1), F32),
            pltpu.VMEM((2, tq, 1), F32),
            pltpu.VMEM((2, tq, DA_V_DIM), F32),
        ],
        compiler_params=pltpu.CompilerParams(
            dimension_semantics=("arbitrary", "arbitrary", "arbitrary"),
            vmem_limit_bytes=VMEM_LIMIT),
        name="diff_attn",
    )(z3, z3, z3, lq1, lk1, lq2, lk2, subln_g)


def _mem_kv_kernel(m_ref, w_ref, o_ref):
    o_ref[...] = _dot(m_ref[...].astype(BF16), w_ref[...]).astype(BF16)


def _mem_kv(mem2, w_b):
    rows, width = mem2.shape[0], w_b.shape[1]
    tn = 1024
    return pl.pallas_call(
        _mem_kv_kernel,
        grid=(width // tn,),
        in_specs=[pl.BlockSpec((rows, D_MODEL), lambda j: (0, 0)),
                  pl.BlockSpec((D_MODEL, tn), lambda j: (0, j))],
        out_specs=pl.BlockSpec((rows, tn), lambda j: (0, j)),
        out_shape=jax.ShapeDtypeStruct((rows, width), BF16),
        compiler_params=pltpu.CompilerParams(
            dimension_semantics=("arbitrary",), vmem_limit_bytes=VMEM_LIMIT),
        name="mem_kv",
    )(mem2, w_b)


def _mix_kernel(u_ref, v_ref, xq_ref, gate_ref, oda_ref, x_ref, mkv_ref, ws_ref, bs_ref,
                wa_ref, wsg_ref, wm_ref, wo_ref, g1_ref, b1_ref, o_ref, osg_ref, oxa_ref, *, tm):
    row = lax.broadcasted_iota(jnp.int32, (SG_CHUNK, SG_CHUNK), 0)
    col = lax.broadcasted_iota(jnp.int32, (SG_CHUNK, SG_CHUNK), 1)
    for g in range(SG_GROUPS):
        ws = jnp.where(col <= row, ws_ref[g], 0.0).astype(BF16)
        bias = bs_ref[:, g:g + 1]
        cs = slice(g * SG_GROUP_DIM, (g + 1) * SG_GROUP_DIM)
        for c in range(tm // SG_CHUNK):
            rs = slice(c * SG_CHUNK, (c + 1) * SG_CHUNK)
            s = _dot(ws, v_ref[rs, cs]) + bias
            osg_ref[rs, cs] = (u_ref[rs, cs].astype(F32) * s).astype(BF16)

    for hd in range(XA_HEADS):
        cs = slice(hd * XA_HEAD_DIM, (hd + 1) * XA_HEAD_DIM)
        vs = slice(XA_HEADS * XA_HEAD_DIM + hd * XA_HEAD_DIM,
                   XA_HEADS * XA_HEAD_DIM + (hd + 1) * XA_HEAD_DIM)
        s = _dot_nt(xq_ref[:, cs], mkv_ref[0, :, cs]) * (XA_HEAD_DIM ** -0.5)
        s = s - jnp.max(s, axis=-1, keepdims=True)
        p = jnp.exp(s)
        p = p / jnp.sum(p, axis=-1, keepdims=True)
        oxa_ref[:, cs] = _dot(p.astype(BF16), mkv_ref[0, :, vs]).astype(BF16)

    d = D_MODEL
    merged = (gate_ref[:, 0:d].astype(F32) * _dot(oda_ref[...], wa_ref[...])
              + gate_ref[:, d:2 * d].astype(F32) * _dot(osg_ref[...], wsg_ref[...])
              + gate_ref[:, 2 * d:3 * d].astype(F32) * _dot(oxa_ref[...], wm_ref[...]))
    y = _dot(merged.astype(BF16), wo_ref[...])
    o_ref[...] = _layer_norm(ALPHA * x_ref[...] + y, g1_ref[...], b1_ref[...])


def _mix(z2, oda2, x2, mkv3, w_s, b_s_t, wa, wsg, wm, wo, ln_g, ln_b, tm, seq):
    n = x2.shape[0]
    d = D_MODEL
    full = lambda shape: pl.BlockSpec(shape, lambda i: (0,) * len(shape))
    return pl.pallas_call(
        functools.partial(_mix_kernel, tm=tm),
        grid=(n // tm,),
        in_specs=[
            pl.BlockSpec((tm, d), lambda i: (i, 3)),
            pl.BlockSpec((tm, d), lambda i: (i, 4)),
            pl.BlockSpec((tm, d), lambda i: (i, 5)),
            pl.BlockSpec((tm, 3 * d), lambda i: (i, 2)),
            pl.BlockSpec((tm, d), lambda i: (i, 0)),
            pl.BlockSpec((tm, d), lambda i: (i, 0)),
            pl.BlockSpec((1, mkv3.shape[1], mkv3.shape[2]), lambda i: ((i * tm) // seq, 0, 0)),
            full(w_s.shape), full(b_s_t.shape),
            full((d, d)), full((d, d)), full((d, d)), full((d, d)),
            full((1, d)), full((1, d)),
        ],
        out_specs=pl.BlockSpec((tm, d), lambda i: (i, 0)),
        out_shape=jax.ShapeDtypeStruct((n, d), F32),
        scratch_shapes=[pltpu.VMEM((tm, d), BF16), pltpu.VMEM((tm, d), BF16)],
        compiler_params=pltpu.CompilerParams(
            dimension_semantics=("arbitrary",), vmem_limit_bytes=VMEM_LIMIT),
        name="mix",
    )(z2, z2, z2, z2, oda2, x2, mkv3, w_s, b_s_t, wa, wsg, wm, wo, ln_g, ln_b)


def _ffn_kernel(x_ref, wi_ref, wo_ref, g_ref, b_ref, o_ref, *, d_ff):
    x = x_ref[...]
    h = _dot(x.astype(BF16), wi_ref[...])
    a = h[:, :d_ff]
    act = (a / (1.0 + jnp.exp(-a))) * h[:, d_ff:]
    y = _dot(act.astype(BF16), wo_ref[...])
    o_ref[...] = _layer_norm(ALPHA * x + y, g_ref[...], b_ref[...])


def _ffn(x1, wi, wo, ln_g, ln_b, tm):
    n, d = x1.shape
    d_ff = wo.shape[0]
    full = lambda shape: pl.BlockSpec(shape, lambda i: (0,) * len(shape))
    return pl.pallas_call(
        functools.partial(_ffn_kernel, d_ff=d_ff),
        grid=(n // tm,),
        in_specs=[pl.BlockSpec((tm, d), lambda i: (i, 0)),
                  full(wi.shape), full(wo.shape), full((1, d)), full((1, d))],
        out_specs=pl.BlockSpec((tm, d), lambda i: (i, 0)),
        out_shape=jax.ShapeDtypeStruct((n, d), F32),
        compiler_params=pltpu.CompilerParams(
            dimension_semantics=("arbitrary",), vmem_limit_bytes=VMEM_LIMIT),
        name="ffn",
    )(x1, wi, wo, ln_g, ln_b)


def kernel(x, mem, positions, w_in, lambda_q1, lambda_k1, lambda_q2, lambda_k2, da_subln_g,
           sg_norm_g, sg_norm_b, sg_w_s, sg_b_s, w_mem_kv, w_br_attn, w_br_sg, w_br_mem, w_out,
           ln1_g, ln1_b, w_ffn_in, w_ffn_out, ln2_g, ln2_b):
    bsz, seq, d = x.shape
    n = bsz * seq
    depth = w_in.shape[0]
    assert d == D_MODEL and depth == DEPTH

    inv_freq = ROPE_THETA ** (-jnp.arange(ROT_HALF, dtype=F32) * 2.0 / ROT_DIM)
    lane_d = jnp.arange(LANES) % DA_HEAD_DIM
    invf = jnp.where(lane_d < ROT_DIM, inv_freq[lane_d % ROT_HALF], 0.0).reshape(1, LANES)
    pos2 = positions.reshape(n, 1)

    x2 = x.reshape(n, d)
    for l in range(depth):
        lambda_init = 0.8 - 0.6 * math.exp(-0.3 * l)
        row = lambda a: a[l].reshape(1, -1).astype(F32)
        z2 = _proj_in(x2, w_in[l].astype(BF16), pos2, invf, row(sg_norm_g), row(sg_norm_b), tm=512)
        oda = _diff_attn(z2.reshape(bsz, seq, -1), row(lambda_q1), row(lambda_k1),
                         row(lambda_q2), row(lambda_k2), row(da_subln_g), tq=256,
                         lambda_init=lambda_init)
        mkv = _mem_kv(mem.reshape(-1, d), w_mem_kv[l].astype(BF16)).reshape(bsz, mem.shape[1], -1)
        x1 = _mix(z2, oda.reshape(n, -1), x2, mkv, sg_w_s[l], sg_b_s[l].T,
                  w_br_attn[l].astype(BF16), w_br_sg[l].astype(BF16), w_br_mem[l].astype(BF16),
                  w_out[l].astype(BF16), row(ln1_g), row(ln1_b), tm=256, seq=seq)
        x2 = _ffn(x1, w_ffn_in[l].astype(BF16), w_ffn_out[l].astype(BF16), row(ln2_g), row(ln2_b),
                  tm=256)
    return x2.reshape(bsz, seq, d)
```

```python
import functools
import math

import jax
import jax.numpy as jnp
from jax import lax
from jax.experimental import pallas as pl
from jax.experimental.pallas import tpu as pltpu

D_MODEL = 1024
DA_HEADS = 8
DA_HEAD_DIM = 64
DA_V_DIM = 2 * DA_HEAD_DIM
ROPE_THETA = 500000.0
ROT_DIM = DA_HEAD_DIM // 4
ROT_HALF = ROT_DIM // 2
SG_GROUPS = 8
SG_CHUNK = 128
SG_GROUP_DIM = 128
XA_HEADS = 4
XA_HEAD_DIM = 256
N_SECTIONS = 9
DEPTH = 1
ALPHA = (2 * DEPTH) ** 0.25
LN_EPS = 1e-5
RMS_EPS = 1e-5
LANES = 128

VMEM_LIMIT = 56 * 1024 * 1024

F32 = jnp.float32
BF16 = jnp.bfloat16


def _dot(a, b):
    return jnp.dot(a, b, preferred_element_type=F32)


def _dot_nt(a, b):
    return lax.dot_general(a, b, (((1,), (1,)), ((), ())), preferred_element_type=F32)


def _layer_norm(x, g, b):
    mu = jnp.mean(x, axis=-1, keepdims=True)
    xc = x - mu
    var = jnp.mean(xc * xc, axis=-1, keepdims=True)
    return xc * lax.rsqrt(var + LN_EPS) * g + b


def _proj_in_kernel(x_ref, w_ref, pos_ref, invf_ref, sgg_ref, sgb_ref, o_ref, xb_ref):
    j = pl.program_id(1)

    @pl.when(j == 0)
    def _():
        xb_ref[...] = x_ref[...].astype(BF16)

    acc = _dot(xb_ref[...], w_ref[...])

    def rope(scale):
        ang = pos_ref[...].astype(F32) * invf_ref[...]
        cos = jnp.cos(ang)
        sin = jnp.sin(ang)
        d = lax.broadcasted_iota(jnp.int32, (1, LANES), 1) % DA_HEAD_DIM
        sin_lo = jnp.where(d < ROT_HALF, -sin, 0.0)
        sin_hi = jnp.where((d >= ROT_HALF) & (d < ROT_DIM), sin, 0.0)
        for h in range(DA_HEADS):
            t = acc[:, h * LANES:(h + 1) * LANES]
            t_up = pltpu.roll(t, LANES - ROT_HALF, 1)
            t_dn = pltpu.roll(t, ROT_HALF, 1)
            r = t * cos + t_up * sin_lo + t_dn * sin_hi
            o_ref[:, h * LANES:(h + 1) * LANES] = (r * scale).astype(BF16)

    @pl.when(j == 0)
    def _():
        rope(DA_HEAD_DIM ** -0.5 * math.log2(math.e))

    @pl.when(j == 1)
    def _():
        rope(1.0)

    @pl.when((j == 2) | (j == 5))
    def _():
        o_ref[...] = acc.astype(BF16)

    @pl.when(j == 3)
    def _():
        o_ref[...] = jax.nn.gelu(acc).astype(BF16)

    @pl.when(j == 4)
    def _():
        o_ref[...] = _layer_norm(jax.nn.gelu(acc), sgg_ref[...], sgb_ref[...]).astype(BF16)

    @pl.when(j >= 6)
    def _():
        o_ref[...] = (1.0 / (1.0 + jnp.exp(-acc))).astype(BF16)


def _proj_in(x2, w_in_b, pos2, invf, sgg, sgb, tm):
    n = x2.shape[0]
    return pl.pallas_call(
        _proj_in_kernel,
        grid=(n // tm, N_SECTIONS),
        in_specs=[
            pl.BlockSpec((tm, D_MODEL), lambda i, j: (i, 0)),
            pl.BlockSpec((D_MODEL, D_MODEL), lambda i, j: (0, j)),
            pl.BlockSpec((tm, 1), lambda i, j: (i, 0)),
            pl.BlockSpec((1, LANES), lambda i, j: (0, 0)),
            pl.BlockSpec((1, D_MODEL), lambda i, j: (0, 0)),
            pl.BlockSpec((1, D_MODEL), lambda i, j: (0, 0)),
        ],
        out_specs=pl.BlockSpec((tm, D_MODEL), lambda i, j: (i, j)),
        out_shape=jax.ShapeDtypeStruct((n, N_SECTIONS * D_MODEL), BF16),
        scratch_shapes=[pltpu.VMEM((tm, D_MODEL), BF16)],
        compiler_params=pltpu.CompilerParams(
            dimension_semantics=("arbitrary", "arbitrary"), vmem_limit_bytes=VMEM_LIMIT),
        name="proj_in",
    )(x2, w_in_b, pos2, invf, sgg, sgb)


def _diff_attn_kernel(q_ref, k_ref, v_ref, lq1_ref, lk1_ref, lq2_ref, lk2_ref, g_ref, o_ref,
                      qs_ref, s_ref, m_ref, l_ref, acc_ref, *, tq, tk, lambda_init):
    i = pl.program_id(2)
    q = q_ref[0]
    lane = lax.broadcasted_iota(jnp.int32, (1, LANES), 1)
    qs_ref[0:tq, :] = jnp.where(lane < DA_HEAD_DIM, q, jnp.zeros_like(q))
    qs_ref[tq:2 * tq, :] = jnp.where(lane >= DA_HEAD_DIM, q, jnp.zeros_like(q))

    rows = 2 * tq
    m_ref[...] = jnp.full(m_ref.shape, -jnp.inf, F32)
    l_ref[...] = jnp.zeros(l_ref.shape, F32)
    acc_ref[...] = jnp.zeros(acc_ref.shape, F32)

    def scores(jb):
        start = pl.multiple_of(jb * tk, tk)
        kb = k_ref[0, pl.ds(start, tk), :]
        return _dot_nt(qs_ref[...], kb)

    def softmax_pv(jb, width, mask_from):
        ngrp = width // LANES
        start = pl.multiple_of(jb * tk, tk)
        vb = v_ref[0, pl.ds(start, width), :]
        row = lax.broadcasted_iota(jnp.int32, (rows, LANES), 0) & (tq - 1)
        col = lax.broadcasted_iota(jnp.int32, (rows, LANES), 1)

        def sgrp(g):
            x = s_ref[:, g * LANES:(g + 1) * LANES]
            if g >= mask_from:
                x = jnp.where(col + (g - mask_from) * LANES <= row, x, -jnp.inf)
            return x

        smax = sgrp(0)
        for g in range(1, ngrp):
            smax = jnp.maximum(smax, sgrp(g))
        m_old = m_ref[...]
        m_new = jnp.maximum(m_old, jnp.max(smax, axis=-1, keepdims=True))
        m_ref[...] = m_new
        a = jnp.exp2(m_old - m_new)
        psum = None
        ps = []
        for g in range(ngrp):
            pg = jnp.exp2(sgrp(g) - m_new)
            psum = pg if psum is None else psum + pg
            ps.append(pg.astype(BF16))
        p = jnp.concatenate(ps, axis=1)
        l_ref[...] = a * l_ref[...] + psum
        acc_ref[...] = a * acc_ref[...] + _dot(p, vb)

    nfull = lax.shift_right_logical(i, 1)
    s_ref[...] = scores(0)

    def body(jb, carry):
        s_next = scores(jb + 1)
        softmax_pv(jb, tk, tk // LANES)
        s_ref[...] = s_next
        return carry

    lax.fori_loop(0, nfull, body, 0)

    @pl.when((i & 1) == 0)
    def _():
        softmax_pv(nfull, tq, 0)

    @pl.when((i & 1) == 1)
    def _():
        softmax_pv(nfull, tk, tq // LANES)

    lam = (jnp.exp(jnp.sum(lq1_ref[...] * lk1_ref[...], axis=-1, keepdims=True))
           - jnp.exp(jnp.sum(lq2_ref[...] * lk2_ref[...], axis=-1, keepdims=True))
           + lambda_init)
    l = jnp.sum(l_ref[...], axis=-1, keepdims=True)
    o = acc_ref[0:tq, :] / l[0:tq] - lam * (acc_ref[tq:2 * tq, :] / l[tq:2 * tq])
    ms = jnp.mean(o * o, axis=-1, keepdims=True)
    o = o * lax.rsqrt(ms + RMS_EPS) * g_ref[...] * (1.0 - lambda_init)
    o_ref[0] = o.astype(BF16)


def _diff_attn(z3, lq1, lk1, lq2, lk2, subln_g, tq, lambda_init):
    bsz, seq, _ = z3.shape
    tk = 2 * tq
    kern = functools.partial(_diff_attn_kernel, tq=tq, tk=tk, lambda_init=lambda_init)
    vec = lambda w: pl.BlockSpec((1, w), lambda b, h, i: (0, 0))
    return pl.pallas_call(
        kern,
        grid=(bsz, DA_HEADS, seq // tq),
        in_specs=[
            pl.BlockSpec((1, tq, LANES), lambda b, h, i: (b, i, h)),
            pl.BlockSpec((1, seq, LANES), lambda b, h, i: (b, 0, DA_HEADS + h)),
            pl.BlockSpec((1, seq, LANES), lambda b, h, i: (b, 0, 2 * DA_HEADS + h)),
            vec(DA_HEAD_DIM), vec(DA_HEAD_DIM), vec(DA_HEAD_DIM), vec(DA_HEAD_DIM),
            vec(DA_V_DIM),
        ],
        out_specs=pl.BlockSpec((1, tq, LANES), lambda b, h, i: (b, i, h)),
        out_shape=jax.ShapeDtypeStruct((bsz, seq, DA_HEADS * DA_V_DIM), BF16),
        scratch_shapes=[
            pltpu.VMEM((2 * tq, LANES), BF16),
            pltpu.VMEM((2 * tq, tk), F32),
            pltpu.VMEM((2 * tq, LANES), F32),
            pltpu.VMEM((2 * tq, LANES), F32),
            pltpu.VMEM((2 * tq, DA_V_DIM), F32),
        ],
        compiler_params=pltpu.CompilerParams(
            dimension_semantics=("arbitrary", "arbitrary", "arbitrary"),
            vmem_limit_bytes=VMEM_LIMIT),
        name="diff_attn",
    )(z3, z3, z3, lq1, lk1, lq2, lk2, subln_g)


def _mem_kv_kernel(m_ref, w_ref, o_ref):
    o_ref[...] = _dot(m_ref[...].astype(BF16), w_ref[...]).astype(BF16)


def _mem_kv(mem2, w_b):
    rows, width = mem2.shape[0], w_b.shape[1]
    tn = 1024
    return pl.pallas_call(
        _mem_kv_kernel,
        grid=(width // tn,),
        in_specs=[pl.BlockSpec((rows, D_MODEL), lambda j: (0, 0)),
                  pl.BlockSpec((D_MODEL, tn), lambda j: (0, j))],
        out_specs=pl.BlockSpec((rows, tn), lambda j: (0, j)),
        out_shape=jax.ShapeDtypeStruct((rows, width), BF16),
        compiler_params=pltpu.CompilerParams(
            dimension_semantics=("arbitrary",), vmem_limit_bytes=VMEM_LIMIT),
        name="mem_kv",
    )(mem2, w_b)


def _mix_kernel(u_ref, v_ref, xq_ref, gate_ref, oda_ref, x_ref, mkv_ref, ws_ref, bs_ref,
                wa_ref, wsg_ref, wm_ref, wo_ref, g1_ref, b1_ref, o_ref, osg_ref, oxa_ref, *, tm):
    row = lax.broadcasted_iota(jnp.int32, (SG_CHUNK, SG_CHUNK), 0)
    col = lax.broadcasted_iota(jnp.int32, (SG_CHUNK, SG_CHUNK), 1)
    for g in range(SG_GROUPS):
        ws = jnp.where(col <= row, ws_ref[g], 0.0).astype(BF16)
        bias = bs_ref[:, g:g + 1]
        cs = slice(g * SG_GROUP_DIM, (g + 1) * SG_GROUP_DIM)
        for c in range(tm // SG_CHUNK):
            rs = slice(c * SG_CHUNK, (c + 1) * SG_CHUNK)
            s = _dot(ws, v_ref[rs, cs]) + bias
            osg_ref[rs, cs] = (u_ref[rs, cs].astype(F32) * s).astype(BF16)

    for hd in range(XA_HEADS):
        cs = slice(hd * XA_HEAD_DIM, (hd + 1) * XA_HEAD_DIM)
        vs = slice(XA_HEADS * XA_HEAD_DIM + hd * XA_HEAD_DIM,
                   XA_HEADS * XA_HEAD_DIM + (hd + 1) * XA_HEAD_DIM)
        s = _dot_nt(xq_ref[:, cs], mkv_ref[0, :, cs]) * (XA_HEAD_DIM ** -0.5)
        s = s - jnp.max(s, axis=-1, keepdims=True)
        p = jnp.exp(s)
        p = p / jnp.sum(p, axis=-1, keepdims=True)
        oxa_ref[:, cs] = _dot(p.astype(BF16), mkv_ref[0, :, vs]).astype(BF16)

    d = D_MODEL
    merged = (gate_ref[:, 0:d].astype(F32) * _dot(oda_ref[...], wa_ref[...])
              + gate_ref[:, d:2 * d].astype(F32) * _dot(osg_ref[...], wsg_ref[...])
              + gate_ref[:, 2 * d:3 * d].astype(F32) * _dot(oxa_ref[...], wm_ref[...]))
    y = _dot(merged.astype(BF16), wo_ref[...])
    o_ref[...] = _layer_norm(ALPHA * x_ref[...] + y, g1_ref[...], b1_ref[...])


def _mix(z2, oda2, x2, mkv3, w_s, b_s_t, wa, wsg, wm, wo, ln_g, ln_b, tm, seq):
    n = x2.shape[0]
    d = D_MODEL
    full = lambda shape: pl.BlockSpec(shape, lambda i: (0,) * len(shape))
    return pl.pallas_call(
        functools.partial(_mix_kernel, tm=tm),
        grid=(n // tm,),
        in_specs=[
            pl.BlockSpec((tm, d), lambda i: (i, 3)),
            pl.BlockSpec((tm, d), lambda i: (i, 4)),
            pl.BlockSpec((tm, d), lambda i: (i, 5)),
            pl.BlockSpec((tm, 3 * d), lambda i: (i, 2)),
            pl.BlockSpec((tm, d), lambda i: (i, 0)),
            pl.BlockSpec((tm, d), lambda i: (i, 0)),
            pl.BlockSpec((1, mkv3.shape[1], mkv3.shape[2]), lambda i: ((i * tm) // seq, 0, 0)),
            full(w_s.shape), full(b_s_t.shape),
            full((d, d)), full((d, d)), full((d, d)), full((d, d)),
            full((1, d)), full((1, d)),
        ],
        out_specs=pl.BlockSpec((tm, d), lambda i: (i, 0)),
        out_shape=jax.ShapeDtypeStruct((n, d), F32),
        scratch_shapes=[pltpu.VMEM((tm, d), BF16), pltpu.VMEM((tm, d), BF16)],
        compiler_params=pltpu.CompilerParams(
            dimension_semantics=("arbitrary",), vmem_limit_bytes=VMEM_LIMIT),
        name="mix",
    )(z2, z2, z2, z2, oda2, x2, mkv3, w_s, b_s_t, wa, wsg, wm, wo, ln_g, ln_b)


def _ffn_kernel(x_ref, wi_ref, wo_ref, g_ref, b_ref, o_ref, *, d_ff):
    x = x_ref[...]
    h = _dot(x.astype(BF16), wi_ref[...])
    a = h[:, :d_ff]
    act = (a / (1.0 + jnp.exp(-a))) * h[:, d_ff:]
    y = _dot(act.astype(BF16), wo_ref[...])
    o_ref[...] = _layer_norm(ALPHA * x + y, g_ref[...], b_ref[...])


def _ffn(x1, wi, wo, ln_g, ln_b, tm):
    n, d = x1.shape
    d_ff = wo.shape[0]
    full = lambda shape: pl.BlockSpec(shape, lambda i: (0,) * len(shape))
    return pl.pallas_call(
        functools.partial(_ffn_kernel, d_ff=d_ff),
        grid=(n // tm,),
        in_specs=[pl.BlockSpec((tm, d), lambda i: (i, 0)),
                  full(wi.shape), full(wo.shape), full((1, d)), full((1, d))],
        out_specs=pl.BlockSpec((tm, d), lambda i: (i, 0)),
        out_shape=jax.ShapeDtypeStruct((n, d), F32),
        compiler_params=pltpu.CompilerParams(
            dimension_semantics=("arbitrary",), vmem_limit_bytes=VMEM_LIMIT),
        name="ffn",
    )(x1, wi, wo, ln_g, ln_b)


def kernel(x, mem, positions, w_in, lambda_q1, lambda_k1, lambda_q2, lambda_k2, da_subln_g,
           sg_norm_g, sg_norm_b, sg_w_s, sg_b_s, w_mem_kv, w_br_attn, w_br_sg, w_br_mem, w_out,
           ln1_g, ln1_b, w_ffn_in, w_ffn_out, ln2_g, ln2_b):
    bsz, seq, d = x.shape
    n = bsz * seq
    depth = w_in.shape[0]
    assert d == D_MODEL and depth == DEPTH

    inv_freq = ROPE_THETA ** (-jnp.arange(ROT_HALF, dtype=F32) * 2.0 / ROT_DIM)
    lane_d = jnp.arange(LANES) % DA_HEAD_DIM
    invf = jnp.where(lane_d < ROT_DIM, inv_freq[lane_d % ROT_HALF], 0.0).reshape(1, LANES)
    pos2 = positions.reshape(n, 1)

    x2 = x.reshape(n, d)
    for l in range(depth):
        lambda_init = 0.8 - 0.6 * math.exp(-0.3 * l)
        row = lambda a: a[l].reshape(1, -1).astype(F32)
        z2 = _proj_in(x2, w_in[l].astype(BF16), pos2, invf, row(sg_norm_g), row(sg_norm_b), tm=512)
        oda = _diff_attn(z2.reshape(bsz, seq, -1), row(lambda_q1), row(lambda_k1),
                         row(lambda_q2), row(lambda_k2), row(da_subln_g), tq=512,
                         lambda_init=lambda_init)
        mkv = _mem_kv(mem.reshape(-1, d), w_mem_kv[l].astype(BF16)).reshape(bsz, mem.shape[1], -1)
        x1 = _mix(z2, oda.reshape(n, -1), x2, mkv, sg_w_s[l], sg_b_s[l].T,
                  w_br_attn[l].astype(BF16), w_br_sg[l].astype(BF16), w_br_mem[l].astype(BF16),
                  w_out[l].astype(BF16), row(ln1_g), row(ln1_b), tm=256, seq=seq)
        x2 = _ffn(x1, w_ffn_in[l].astype(BF16), w_ffn_out[l].astype(BF16), row(ln2_g), row(ln2_b),
                  tm=256)
    return x2.reshape(bsz, seq, d)
```

```python
import functools
import math

import jax
import jax.numpy as jnp
from jax import lax
from jax.experimental import pallas as pl
from jax.experimental.pallas import tpu as pltpu

D_MODEL = 1024
DA_HEADS = 8
DA_HEAD_DIM = 64
DA_V_DIM = 2 * DA_HEAD_DIM
ROPE_THETA = 500000.0
ROT_DIM = DA_HEAD_DIM // 4
ROT_HALF = ROT_DIM // 2
SG_GROUPS = 8
SG_CHUNK = 128
SG_GROUP_DIM = 128
XA_HEADS = 4
XA_HEAD_DIM = 256
N_SECTIONS = 9
DEPTH = 1
ALPHA = (2 * DEPTH) ** 0.25
LN_EPS = 1e-5
RMS_EPS = 1e-5
LANES = 128
COL_CHUNK = 256

VMEM_LIMIT = 56 * 1024 * 1024

F32 = jnp.float32
BF16 = jnp.bfloat16


def _dot(a, b):
    return jnp.dot(a, b, preferred_element_type=F32)


def _dot_nt(a, b):
    return lax.dot_general(a, b, (((1,), (1,)), ((), ())), preferred_element_type=F32)


def _sigmoid(x):
    return 0.5 + 0.5 * jnp.tanh(0.5 * x)


def _layer_norm(x, g, b):
    mu = jnp.mean(x, axis=-1, keepdims=True)
    xc = x - mu
    var = jnp.mean(xc * xc, axis=-1, keepdims=True)
    return xc * lax.rsqrt(var + LN_EPS) * g + b


def _proj_in_kernel(x_ref, w_ref, pos_ref, invf_ref, sgg_ref, sgb_ref, o_ref,
                    xb_ref, cos_ref, sin_ref, gel_ref):
    j = pl.program_id(1)

    @pl.when(j == 0)
    def _():
        xb_ref[...] = x_ref[...].astype(BF16)
        ang = pos_ref[...].astype(F32) * invf_ref[...]
        cos_ref[...] = jnp.cos(ang)
        sin_ref[...] = jnp.sin(ang)

    def run(epilogue):
        for c in range(D_MODEL // COL_CHUNK):
            cs = slice(c * COL_CHUNK, (c + 1) * COL_CHUNK)
            epilogue(_dot(xb_ref[...], w_ref[:, cs]), cs)

    def rope(scale):
        d = lax.broadcasted_iota(jnp.int32, (1, LANES), 1) % DA_HEAD_DIM
        sgn_lo = jnp.where(d < ROT_HALF, -scale, 0.0)
        sgn_hi = jnp.where((d >= ROT_HALF) & (d < ROT_DIM), scale, 0.0)

        def epilogue(acc, cs):
            cos = cos_ref[...] * scale
            sin_lo = sin_ref[...] * sgn_lo
            sin_hi = sin_ref[...] * sgn_hi
            for h in range(COL_CHUNK // LANES):
                t = acc[:, h * LANES:(h + 1) * LANES]
                t_up = pltpu.roll(t, LANES - ROT_HALF, 1)
                t_dn = pltpu.roll(t, ROT_HALF, 1)
                r = t * cos + t_up * sin_lo + t_dn * sin_hi
                o_ref[:, cs.start + h * LANES:cs.start + (h + 1) * LANES] = r.astype(BF16)
        return epilogue

    @pl.when(j == 0)
    def _():
        run(rope(DA_HEAD_DIM ** -0.5 * math.log2(math.e)))

    @pl.when(j == 1)
    def _():
        run(rope(1.0))

    @pl.when((j == 2) | (j == 5))
    def _():
        def epilogue(acc, cs):
            o_ref[:, cs] = acc.astype(BF16)
        run(epilogue)

    @pl.when(j == 3)
    def _():
        def epilogue(acc, cs):
            o_ref[:, cs] = jax.nn.gelu(acc).astype(BF16)
        run(epilogue)

    @pl.when(j == 4)
    def _():
        def epilogue(acc, cs):
            gel_ref[:, cs] = jax.nn.gelu(acc)
        run(epilogue)
        o_ref[...] = _layer_norm(gel_ref[...], sgg_ref[...], sgb_ref[...]).astype(BF16)

    @pl.when(j >= 6)
    def _():
        def epilogue(acc, cs):
            o_ref[:, cs] = _sigmoid(acc).astype(BF16)
        run(epilogue)


def _proj_in(x2, w_in_b, pos2, invf, sgg, sgb, tm):
    n = x2.shape[0]
    return pl.pallas_call(
        _proj_in_kernel,
        grid=(n // tm, N_SECTIONS),
        in_specs=[
            pl.BlockSpec((tm, D_MODEL), lambda i, j: (i, 0)),
            pl.BlockSpec((D_MODEL, D_MODEL), lambda i, j: (0, j)),
            pl.BlockSpec((tm, 1), lambda i, j: (i, 0)),
            pl.BlockSpec((1, LANES), lambda i, j: (0, 0)),
            pl.BlockSpec((1, D_MODEL), lambda i, j: (0, 0)),
            pl.BlockSpec((1, D_MODEL), lambda i, j: (0, 0)),
        ],
        out_specs=pl.BlockSpec((tm, D_MODEL), lambda i, j: (i, j)),
        out_shape=jax.ShapeDtypeStruct((n, N_SECTIONS * D_MODEL), BF16),
        scratch_shapes=[
            pltpu.VMEM((tm, D_MODEL), BF16),
            pltpu.VMEM((tm, LANES), F32),
            pltpu.VMEM((tm, LANES), F32),
            pltpu.VMEM((tm, D_MODEL), F32),
        ],
        compiler_params=pltpu.CompilerParams(
            dimension_semantics=("arbitrary", "arbitrary"), vmem_limit_bytes=VMEM_LIMIT),
        name="proj_in",
    )(x2, w_in_b, pos2, invf, sgg, sgb)


def _diff_attn_kernel(q_ref, k_ref, v_ref, lq1_ref, lk1_ref, lq2_ref, lk2_ref, g_ref, o_ref,
                      qs_ref, s_ref, mx_ref, m_ref, l_ref, acc_ref, *, tq, tk, lambda_init):
    i = pl.program_id(2)
    q = q_ref[0]
    lane = lax.broadcasted_iota(jnp.int32, (1, LANES), 1)
    qs_ref[0:tq, :] = jnp.where(lane < DA_HEAD_DIM, q, jnp.zeros_like(q))
    qs_ref[tq:2 * tq, :] = jnp.where(lane >= DA_HEAD_DIM, q, jnp.zeros_like(q))

    rows = 2 * tq
    m_ref[...] = jnp.full(m_ref.shape, -jnp.inf, F32)
    l_ref[...] = jnp.zeros(l_ref.shape, F32)
    acc_ref[...] = jnp.zeros(acc_ref.shape, F32)

    def scores(jb):
        start = pl.multiple_of(jb * tk, tk)
        kb = k_ref[0, pl.ds(start, tk), :]
        s = _dot_nt(qs_ref[...], kb)
        smax = s[:, 0:LANES]
        for g in range(1, tk // LANES):
            smax = jnp.maximum(smax, s[:, g * LANES:(g + 1) * LANES])
        return s, smax

    def softmax_pv(jb, slot, width, mask_from, smax=None):
        ngrp = width // LANES
        start = pl.multiple_of(jb * tk, tk)
        vb = v_ref[0, pl.ds(start, width), :]
        row = lax.broadcasted_iota(jnp.int32, (rows, LANES), 0) & (tq - 1)
        col = lax.broadcasted_iota(jnp.int32, (rows, LANES), 1)

        def sgrp(g):
            x = s_ref[slot, :, g * LANES:(g + 1) * LANES]
            if g >= mask_from:
                x = jnp.where(col + (g - mask_from) * LANES <= row, x, -jnp.inf)
            return x

        if smax is None:
            smax = sgrp(0)
            for g in range(1, ngrp):
                smax = jnp.maximum(smax, sgrp(g))
        m_old = m_ref[...]
        m_new = jnp.maximum(m_old, jnp.max(smax, axis=-1, keepdims=True))
        m_ref[...] = m_new
        a = jnp.exp2(m_old - m_new)
        psum = None
        ps = []
        for g in range(ngrp):
            pg = jnp.exp2(sgrp(g) - m_new)
            psum = pg if psum is None else psum + pg
            ps.append(pg.astype(BF16))
        p = jnp.concatenate(ps, axis=1)
        l_ref[...] = a * l_ref[...] + psum
        acc_ref[...] = a * acc_ref[...] + _dot(p, vb)

    nfull = lax.shift_right_logical(i, 1)
    s_ref[0], mx_ref[0] = scores(0)

    def body(jb, carry):
        for slot in range(2):
            @pl.when((jb & 1) == slot)
            def _():
                s_ref[1 - slot], mx_ref[1 - slot] = scores(jb + 1)
                softmax_pv(jb, slot, tk, tk // LANES, smax=mx_ref[slot])
        return carry

    lax.fori_loop(0, nfull, body, 0)

    for slot in range(2):
        @pl.when(((i & 1) == 0) & ((nfull & 1) == slot))
        def _():
            softmax_pv(nfull, slot, tq, 0)

        @pl.when(((i & 1) == 1) & ((nfull & 1) == slot))
        def _():
            softmax_pv(nfull, slot, tk, tq // LANES)

    lam = (jnp.exp(jnp.sum(lq1_ref[...] * lk1_ref[...], axis=-1, keepdims=True))
           - jnp.exp(jnp.sum(lq2_ref[...] * lk2_ref[...], axis=-1, keepdims=True))
           + lambda_init)
    l = jnp.sum(l_ref[...], axis=-1, keepdims=True)
    o = acc_ref[0:tq, :] / l[0:tq] - lam * (acc_ref[tq:2 * tq, :] / l[tq:2 * tq])
    ms = jnp.mean(o * o, axis=-1, keepdims=True)
    o = o * lax.rsqrt(ms + RMS_EPS) * g_ref[...] * (1.0 - lambda_init)
    o_ref[0] = o.astype(BF16)


def _diff_attn(z3, lq1, lk1, lq2, lk2, subln_g, tq, lambda_init):
    bsz, seq, _ = z3.shape
    tk = 2 * tq
    kern = functools.partial(_diff_attn_kernel, tq=tq, tk=tk, lambda_init=lambda_init)
    vec = lambda w: pl.BlockSpec((1, w), lambda b, h, i: (0, 0))
    return pl.pallas_call(
        kern,
        grid=(bsz, DA_HEADS, seq // tq),
        in_specs=[
            pl.BlockSpec((1, tq, LANES), lambda b, h, i: (b, i, h)),
            pl.BlockSpec((1, seq, LANES), lambda b, h, i: (b, 0, DA_HEADS + h)),
            pl.BlockSpec((1, seq, LANES), lambda b, h, i: (b, 0, 2 * DA_HEADS + h)),
            vec(DA_HEAD_DIM), vec(DA_HEAD_DIM), vec(DA_HEAD_DIM), vec(DA_HEAD_DIM),
            vec(DA_V_DIM),
        ],
        out_specs=pl.BlockSpec((1, tq, LANES), lambda b, h, i: (b, i, h)),
        out_shape=jax.ShapeDtypeStruct((bsz, seq, DA_HEADS * DA_V_DIM), BF16),
        scratch_shapes=[
            pltpu.VMEM((2 * tq, LANES), BF16),
            pltpu.VMEM((2, 2 * tq, tk), F32),
            pltpu.VMEM((2, 2 * tq, LANES), F32),
            pltpu.VMEM((2 * tq, LANES), F32),
            pltpu.VMEM((2 * tq, LANES), F32),
            pltpu.VMEM((2 * tq, DA_V_DIM), F32),
        ],
        compiler_params=pltpu.CompilerParams(
            dimension_semantics=("arbitrary", "arbitrary", "arbitrary"),
            vmem_limit_bytes=VMEM_LIMIT),
        name="diff_attn",
    )(z3, z3, z3, lq1, lk1, lq2, lk2, subln_g)


def _mem_kv_kernel(m_ref, w_ref, o_ref):
    o_ref[...] = _dot(m_ref[...].astype(BF16), w_ref[...]).astype(BF16)


def _mem_kv(mem2, w_b):
    rows, width = mem2.shape[0], w_b.shape[1]
    tn = 1024
    return pl.pallas_call(
        _mem_kv_kernel,
        grid=(width // tn,),
        in_specs=[pl.BlockSpec((rows, D_MODEL), lambda j: (0, 0)),
                  pl.BlockSpec((D_MODEL, tn), lambda j: (0, j))],
        out_specs=pl.BlockSpec((rows, tn), lambda j: (0, j)),
        out_shape=jax.ShapeDtypeStruct((rows, width), BF16),
        compiler_params=pltpu.CompilerParams(
            dimension_semantics=("arbitrary",), vmem_limit_bytes=VMEM_LIMIT),
        name="mem_kv",
    )(mem2, w_b)


def _mix_kernel(u_ref, v_ref, xq_ref, gate_ref, oda_ref, x_ref, mkv_ref, ws_ref, bs_ref,
                wa_ref, wsg_ref, wm_ref, wo_ref, g1_ref, b1_ref, o_ref, osg_ref, oxa_ref, *, tm):
    row = lax.broadcasted_iota(jnp.int32, (SG_CHUNK, SG_CHUNK), 0)
    col = lax.broadcasted_iota(jnp.int32, (SG_CHUNK, SG_CHUNK), 1)
    for g in range(SG_GROUPS):
        ws = jnp.where(col <= row, ws_ref[g], 0.0).astype(BF16)
        bias = bs_ref[:, g:g + 1]
        cs = slice(g * SG_GROUP_DIM, (g + 1) * SG_GROUP_DIM)
        for c in range(tm // SG_CHUNK):
            rs = slice(c * SG_CHUNK, (c + 1) * SG_CHUNK)
            s = _dot(ws, v_ref[rs, cs]) + bias
            osg_ref[rs, cs] = (u_ref[rs, cs].astype(F32) * s).astype(BF16)

    for hd in range(XA_HEADS):
        cs = slice(hd * XA_HEAD_DIM, (hd + 1) * XA_HEAD_DIM)
        vs = slice(XA_HEADS * XA_HEAD_DIM + hd * XA_HEAD_DIM,
                   XA_HEADS * XA_HEAD_DIM + (hd + 1) * XA_HEAD_DIM)
        s = _dot_nt(xq_ref[:, cs], mkv_ref[0, :, cs]) * (XA_HEAD_DIM ** -0.5)
        s = s - jnp.max(s, axis=-1, keepdims=True)
        p = jnp.exp(s)
        p = p / jnp.sum(p, axis=-1, keepdims=True)
        oxa_ref[:, cs] = _dot(p.astype(BF16), mkv_ref[0, :, vs]).astype(BF16)

    d = D_MODEL
    merged = (gate_ref[:, 0:d].astype(F32) * _dot(oda_ref[...], wa_ref[...])
              + gate_ref[:, d:2 * d].astype(F32) * _dot(osg_ref[...], wsg_ref[...])
              + gate_ref[:, 2 * d:3 * d].astype(F32) * _dot(oxa_ref[...], wm_ref[...]))
    y = _dot(merged.astype(BF16), wo_ref[...])
    o_ref[...] = _layer_norm(ALPHA * x_ref[...] + y, g1_ref[...], b1_ref[...])


def _mix(z2, oda2, x2, mkv3, w_s, b_s_t, wa, wsg, wm, wo, ln_g, ln_b, tm, seq):
    n = x2.shape[0]
    d = D_MODEL
    full = lambda shape: pl.BlockSpec(shape, lambda i: (0,) * len(shape))
    return pl.pallas_call(
        functools.partial(_mix_kernel, tm=tm),
        grid=(n // tm,),
        in_specs=[
            pl.BlockSpec((tm, d), lambda i: (i, 3)),
            pl.BlockSpec((tm, d), lambda i: (i, 4)),
            pl.BlockSpec((tm, d), lambda i: (i, 5)),
            pl.BlockSpec((tm, 3 * d), lambda i: (i, 2)),
            pl.BlockSpec((tm, d), lambda i: (i, 0)),
            pl.BlockSpec((tm, d), lambda i: (i, 0)),
            pl.BlockSpec((1, mkv3.shape[1], mkv3.shape[2]), lambda i: ((i * tm) // seq, 0, 0)),
            full(w_s.shape), full(b_s_t.shape),
            full((d, d)), full((d, d)), full((d, d)), full((d, d)),
            full((1, d)), full((1, d)),
        ],
        out_specs=pl.BlockSpec((tm, d), lambda i: (i, 0)),
        out_shape=jax.ShapeDtypeStruct((n, d), F32),
        scratch_shapes=[pltpu.VMEM((tm, d), BF16), pltpu.VMEM((tm, d), BF16)],
        compiler_params=pltpu.CompilerParams(
            dimension_semantics=("arbitrary",), vmem_limit_bytes=VMEM_LIMIT),
        name="mix",
    )(z2, z2, z2, z2, oda2, x2, mkv3, w_s, b_s_t, wa, wsg, wm, wo, ln_g, ln_b)


def _ffn_kernel(x_ref, wi_ref, wo_ref, g_ref, b_ref, o_ref, *, d_ff):
    x = x_ref[...]
    h = _dot(x.astype(BF16), wi_ref[...])
    a = h[:, :d_ff]
    act = (a * _sigmoid(a)) * h[:, d_ff:]
    y = _dot(act.astype(BF16), wo_ref[...])
    o_ref[...] = _layer_norm(ALPHA * x + y, g_ref[...], b_ref[...])


def _ffn(x1, wi, wo, ln_g, ln_b, tm):
    n, d = x1.shape
    d_ff = wo.shape[0]
    full = lambda shape: pl.BlockSpec(shape, lambda i: (0,) * len(shape))
    return pl.pallas_call(
        functools.partial(_ffn_kernel, d_ff=d_ff),
        grid=(n // tm,),
        in_specs=[pl.BlockSpec((tm, d), lambda i: (i, 0)),
                  full(wi.shape), full(wo.shape), full((1, d)), full((1, d))],
        out_specs=pl.BlockSpec((tm, d), lambda i: (i, 0)),
        out_shape=jax.ShapeDtypeStruct((n, d), F32),
        compiler_params=pltpu.CompilerParams(
            dimension_semantics=("arbitrary",), vmem_limit_bytes=VMEM_LIMIT),
        name="ffn",
    )(x1, wi, wo, ln_g, ln_b)


def kernel(x, mem, positions, w_in, lambda_q1, lambda_k1, lambda_q2, lambda_k2, da_subln_g,
           sg_norm_g, sg_norm_b, sg_w_s, sg_b_s, w_mem_kv, w_br_attn, w_br_sg, w_br_mem, w_out,
           ln1_g, ln1_b, w_ffn_in, w_ffn_out, ln2_g, ln2_b):
    bsz, seq, d = x.shape
    n = bsz * seq
    depth = w_in.shape[0]
    assert d == D_MODEL and depth == DEPTH

    inv_freq = ROPE_THETA ** (-jnp.arange(ROT_HALF, dtype=F32) * 2.0 / ROT_DIM)
    lane_d = jnp.arange(LANES) % DA_HEAD_DIM
    invf = jnp.where(lane_d < ROT_DIM, inv_freq[lane_d % ROT_HALF], 0.0).reshape(1, LANES)
    pos2 = positions.reshape(n, 1)

    x2 = x.reshape(n, d)
    for l in range(depth):
        lambda_init = 0.8 - 0.6 * math.exp(-0.3 * l)
        row = lambda a: a[l].reshape(1, -1).astype(F32)
        z2 = _proj_in(x2, w_in[l].astype(BF16), pos2, invf, row(sg_norm_g), row(sg_norm_b), tm=512)
        oda = _diff_attn(z2.reshape(bsz, seq, -1), row(lambda_q1), row(lambda_k1),
                         row(lambda_q2), row(lambda_k2), row(da_subln_g), tq=512,
                         lambda_init=lambda_init)
        mkv = _mem_kv(mem.reshape(-1, d), w_mem_kv[l].astype(BF16)).reshape(bsz, mem.shape[1], -1)
        x1 = _mix(z2, oda.reshape(n, -1), x2, mkv, sg_w_s[l], sg_b_s[l].T,
                  w_br_attn[l].astype(BF16), w_br_sg[l].astype(BF16), w_br_mem[l].astype(BF16),
                  w_out[l].astype(BF16), row(ln1_g), row(ln1_b), tm=256, seq=seq)
        x2 = _ffn(x1, w_ffn_in[l].astype(BF16), w_ffn_out[l].astype(BF16), row(ln2_g), row(ln2_b),
                  tm=256)
    return x2.reshape(bsz, seq, d)
```

```python
import functools
import math

import jax
import jax.numpy as jnp
from jax import lax
from jax.experimental import pallas as pl
from jax.experimental.pallas import tpu as pltpu

D_MODEL = 1024
DA_HEADS = 8
DA_HEAD_DIM = 64
DA_V_DIM = 2 * DA_HEAD_DIM
ROPE_THETA = 500000.0
ROT_DIM = DA_HEAD_DIM // 4
ROT_HALF = ROT_DIM // 2
SG_GROUPS = 8
SG_CHUNK = 128
SG_GROUP_DIM = 128
XA_HEADS = 4
XA_HEAD_DIM = 256
N_SECTIONS = 9
DEPTH = 1
ALPHA = (2 * DEPTH) ** 0.25
LN_EPS = 1e-5
RMS_EPS = 1e-5
LANES = 128
COL_CHUNK = 256

VMEM_LIMIT = 56 * 1024 * 1024

F32 = jnp.float32
BF16 = jnp.bfloat16


def _dot(a, b):
    return jnp.dot(a, b, preferred_element_type=F32)


def _dot_nt(a, b):
    return lax.dot_general(a, b, (((1,), (1,)), ((), ())), preferred_element_type=F32)


def _sigmoid(x):
    return 0.5 + 0.5 * jnp.tanh(0.5 * x)


def _layer_norm(x, g, b):
    mu = jnp.mean(x, axis=-1, keepdims=True)
    xc = x - mu
    var = jnp.mean(xc * xc, axis=-1, keepdims=True)
    return xc * lax.rsqrt(var + LN_EPS) * g + b


def _proj_in_kernel(x_ref, w_ref, pos_ref, invf_ref, sgg_ref, sgb_ref, o_ref,
                    xb_ref, cos_ref, sin_ref, gel_ref):
    j = pl.program_id(1)

    @pl.when(j == 0)
    def _():
        xb_ref[...] = x_ref[...].astype(BF16)
        ang = pos_ref[...].astype(F32) * invf_ref[...]
        cos_ref[...] = jnp.cos(ang)
        sin_ref[...] = jnp.sin(ang)

    def run(epilogue):
        for c in range(D_MODEL // COL_CHUNK):
            cs = slice(c * COL_CHUNK, (c + 1) * COL_CHUNK)
            epilogue(_dot(xb_ref[...], w_ref[:, cs]), cs)

    def rope(scale):
        d = lax.broadcasted_iota(jnp.int32, (1, LANES), 1) % DA_HEAD_DIM
        sgn_lo = jnp.where(d < ROT_HALF, -scale, 0.0)
        sgn_hi = jnp.where((d >= ROT_HALF) & (d < ROT_DIM), scale, 0.0)

        def epilogue(acc, cs):
            cos = cos_ref[...] * scale
            sin_lo = sin_ref[...] * sgn_lo
            sin_hi = sin_ref[...] * sgn_hi
            for h in range(COL_CHUNK // LANES):
                t = acc[:, h * LANES:(h + 1) * LANES]
                t_up = pltpu.roll(t, LANES - ROT_HALF, 1)
                t_dn = pltpu.roll(t, ROT_HALF, 1)
                r = t * cos + t_up * sin_lo + t_dn * sin_hi
                o_ref[:, cs.start + h * LANES:cs.start + (h + 1) * LANES] = r.astype(BF16)
        return epilogue

    @pl.when(j == 0)
    def _():
        run(rope(DA_HEAD_DIM ** -0.5 * math.log2(math.e)))

    @pl.when(j == 1)
    def _():
        run(rope(1.0))

    @pl.when((j == 2) | (j == 5))
    def _():
        def epilogue(acc, cs):
            o_ref[:, cs] = acc.astype(BF16)
        run(epilogue)

    @pl.when(j == 3)
    def _():
        def epilogue(acc, cs):
            o_ref[:, cs] = jax.nn.gelu(acc).astype(BF16)
        run(epilogue)

    @pl.when(j == 4)
    def _():
        def epilogue(acc, cs):
            gel_ref[:, cs] = jax.nn.gelu(acc)
        run(epilogue)
        o_ref[...] = _layer_norm(gel_ref[...], sgg_ref[...], sgb_ref[...]).astype(BF16)

    @pl.when(j >= 6)
    def _():
        def epilogue(acc, cs):
            o_ref[:, cs] = _sigmoid(acc).astype(BF16)
        run(epilogue)


def _proj_in(x2, w_in_b, pos2, invf, sgg, sgb, tm):
    n = x2.shape[0]
    return pl.pallas_call(
        _proj_in_kernel,
        grid=(n // tm, N_SECTIONS),
        in_specs=[
            pl.BlockSpec((tm, D_MODEL), lambda i, j: (i, 0)),
            pl.BlockSpec((D_MODEL, D_MODEL), lambda i, j: (0, j)),
            pl.BlockSpec((tm, 1), lambda i, j: (i, 0)),
            pl.BlockSpec((1, LANES), lambda i, j: (0, 0)),
            pl.BlockSpec((1, D_MODEL), lambda i, j: (0, 0)),
            pl.BlockSpec((1, D_MODEL), lambda i, j: (0, 0)),
        ],
        out_specs=pl.BlockSpec((tm, D_MODEL), lambda i, j: (i, j)),
        out_shape=jax.ShapeDtypeStruct((n, N_SECTIONS * D_MODEL), BF16),
        scratch_shapes=[
            pltpu.VMEM((tm, D_MODEL), BF16),
            pltpu.VMEM((tm, LANES), F32),
            pltpu.VMEM((tm, LANES), F32),
            pltpu.VMEM((tm, D_MODEL), F32),
        ],
        compiler_params=pltpu.CompilerParams(
            dimension_semantics=("arbitrary", "arbitrary"), vmem_limit_bytes=VMEM_LIMIT),
        name="proj_in",
    )(x2, w_in_b, pos2, invf, sgg, sgb)


def _diff_attn_kernel(q_ref, k_ref, v_ref, lq1_ref, lk1_ref, lq2_ref, lk2_ref, g_ref, o_ref,
                      qs_ref, s0_ref, s1_ref, mx0_ref, mx1_ref, m_ref, l_ref, acc_ref,
                      *, tq, tk, lambda_init):
    s_ref = (s0_ref, s1_ref)
    mx_ref = (mx0_ref, mx1_ref)
    i = pl.program_id(2)
    q = q_ref[0]
    lane = lax.broadcasted_iota(jnp.int32, (1, LANES), 1)
    qs_ref[0:tq, :] = jnp.where(lane < DA_HEAD_DIM, q, jnp.zeros_like(q))
    qs_ref[tq:2 * tq, :] = jnp.where(lane >= DA_HEAD_DIM, q, jnp.zeros_like(q))

    rows = 2 * tq
    m_ref[...] = jnp.full(m_ref.shape, -jnp.inf, F32)
    l_ref[...] = jnp.zeros(l_ref.shape, F32)
    acc_ref[...] = jnp.zeros(acc_ref.shape, F32)

    def scores(jb):
        start = pl.multiple_of(jb * tk, tk)
        kb = k_ref[0, pl.ds(start, tk), :]
        s = _dot_nt(qs_ref[...], kb)
        smax = s[:, 0:LANES]
        for g in range(1, tk // LANES):
            smax = jnp.maximum(smax, s[:, g * LANES:(g + 1) * LANES])
        return s, smax

    def softmax_pv(jb, slot, width, mask_from, smax=None):
        ngrp = width // LANES
        start = pl.multiple_of(jb * tk, tk)
        vb = v_ref[0, pl.ds(start, width), :]
        row = lax.broadcasted_iota(jnp.int32, (rows, LANES), 0) & (tq - 1)
        col = lax.broadcasted_iota(jnp.int32, (rows, LANES), 1)

        def sgrp(g):
            x = s_ref[slot][:, g * LANES:(g + 1) * LANES]
            if g >= mask_from:
                x = jnp.where(col + (g - mask_from) * LANES <= row, x, -jnp.inf)
            return x

        if smax is None:
            smax = sgrp(0)
            for g in range(1, ngrp):
                smax = jnp.maximum(smax, sgrp(g))
        m_old = m_ref[...]
        m_new = jnp.maximum(m_old, jnp.max(smax, axis=-1, keepdims=True))
        m_ref[...] = m_new
        a = jnp.exp2(m_old - m_new)
        p = jnp.concatenate([jnp.exp2((sgrp(g) - m_new).astype(BF16)) for g in range(ngrp)],
                            axis=1)
        v1 = jnp.concatenate([vb, jnp.ones((width, LANES), BF16)], axis=1)
        pv = _dot(p, v1)
        acc_ref[...] = a * acc_ref[...] + pv[:, :DA_V_DIM]
        l_ref[...] = a * l_ref[...] + pv[:, DA_V_DIM:]

    nfull = lax.shift_right_logical(i, 1)
    s_ref[0][...], mx_ref[0][...] = scores(0)

    def body(jb, carry):
        for slot in range(2):
            @pl.when((jb & 1) == slot)
            def _():
                s_ref[1 - slot][...], mx_ref[1 - slot][...] = scores(jb + 1)
                softmax_pv(jb, slot, tk, tk // LANES, smax=mx_ref[slot][...])
        return carry

    lax.fori_loop(0, nfull, body, 0)

    for slot in range(2):
        @pl.when(((i & 1) == 0) & ((nfull & 1) == slot))
        def _():
            softmax_pv(nfull, slot, tq, 0)

        @pl.when(((i & 1) == 1) & ((nfull & 1) == slot))
        def _():
            softmax_pv(nfull, slot, tk, tq // LANES)

    lam = (jnp.exp(jnp.sum(lq1_ref[...] * lk1_ref[...], axis=-1, keepdims=True))
           - jnp.exp(jnp.sum(lq2_ref[...] * lk2_ref[...], axis=-1, keepdims=True))
           + lambda_init)
    o = (acc_ref[0:tq, :] / l_ref[0:tq, :]
         - lam * (acc_ref[tq:2 * tq, :] / l_ref[tq:2 * tq, :]))
    ms = jnp.mean(o * o, axis=-1, keepdims=True)
    o = o * lax.rsqrt(ms + RMS_EPS) * g_ref[...] * (1.0 - lambda_init)
    o_ref[0] = o.astype(BF16)


def _diff_attn(z3, lq1, lk1, lq2, lk2, subln_g, tq, lambda_init):
    bsz, seq, _ = z3.shape
    tk = 2 * tq
    kern = functools.partial(_diff_attn_kernel, tq=tq, tk=tk, lambda_init=lambda_init)
    vec = lambda w: pl.BlockSpec((1, w), lambda b, h, i: (0, 0))
    return pl.pallas_call(
        kern,
        grid=(bsz, DA_HEADS, seq // tq),
        in_specs=[
            pl.BlockSpec((1, tq, LANES), lambda b, h, i: (b, i, h)),
            pl.BlockSpec((1, seq, LANES), lambda b, h, i: (b, 0, DA_HEADS + h)),
            pl.BlockSpec((1, seq, LANES), lambda b, h, i: (b, 0, 2 * DA_HEADS + h)),
            vec(DA_HEAD_DIM), vec(DA_HEAD_DIM), vec(DA_HEAD_DIM), vec(DA_HEAD_DIM),
            vec(DA_V_DIM),
        ],
        out_specs=pl.BlockSpec((1, tq, LANES), lambda b, h, i: (b, i, h)),
        out_shape=jax.ShapeDtypeStruct((bsz, seq, DA_HEADS * DA_V_DIM), BF16),
        scratch_shapes=[
            pltpu.VMEM((2 * tq, LANES), BF16),
            pltpu.VMEM((2 * tq, tk), F32),
            pltpu.VMEM((2 * tq, tk), F32),
            pltpu.VMEM((2 * tq, LANES), F32),
            pltpu.VMEM((2 * tq, LANES), F32),
            pltpu.VMEM((2 * tq, LANES), F32),
            pltpu.VMEM((2 * tq, LANES), F32),
            pltpu.VMEM((2 * tq, DA_V_DIM), F32),
        ],
        compiler_params=pltpu.CompilerParams(
            dimension_semantics=("arbitrary", "arbitrary", "arbitrary"),
            vmem_limit_bytes=VMEM_LIMIT),
        name="diff_attn",
    )(z3, z3, z3, lq1, lk1, lq2, lk2, subln_g)


def _mem_kv_kernel(m_ref, w_ref, o_ref):
    o_ref[...] = _dot(m_ref[...].astype(BF16), w_ref[...]).astype(BF16)


def _mem_kv(mem2, w_b):
    rows, width = mem2.shape[0], w_b.shape[1]
    tn = 1024
    return pl.pallas_call(
        _mem_kv_kernel,
        grid=(width // tn,),
        in_specs=[pl.BlockSpec((rows, D_MODEL), lambda j: (0, 0)),
                  pl.BlockSpec((D_MODEL, tn), lambda j: (0, j))],
        out_specs=pl.BlockSpec((rows, tn), lambda j: (0, j)),
        out_shape=jax.ShapeDtypeStruct((rows, width), BF16),
        compiler_params=pltpu.CompilerParams(
            dimension_semantics=("arbitrary",), vmem_limit_bytes=VMEM_LIMIT),
        name="mem_kv",
    )(mem2, w_b)


def _mix_kernel(u_ref, v_ref, xq_ref, gate_ref, oda_ref, x_ref, mkv_ref, ws_ref, bs_ref,
                wa_ref, wsg_ref, wm_ref, wo_ref, g1_ref, b1_ref, o_ref, osg_ref, oxa_ref, *, tm):
    row = lax.broadcasted_iota(jnp.int32, (SG_CHUNK, SG_CHUNK), 0)
    col = lax.broadcasted_iota(jnp.int32, (SG_CHUNK, SG_CHUNK), 1)
    for g in range(SG_GROUPS):
        ws = jnp.where(col <= row, ws_ref[g], 0.0).astype(BF16)
        bias = bs_ref[:, g:g + 1]
        cs = slice(g * SG_GROUP_DIM, (g + 1) * SG_GROUP_DIM)
        for c in range(tm // SG_CHUNK):
            rs = slice(c * SG_CHUNK, (c + 1) * SG_CHUNK)
            s = _dot(ws, v_ref[rs, cs]) + bias
            osg_ref[rs, cs] = (u_ref[rs, cs].astype(F32) * s).astype(BF16)

    for hd in range(XA_HEADS):
        cs = slice(hd * XA_HEAD_DIM, (hd + 1) * XA_HEAD_DIM)
        vs = slice(XA_HEADS * XA_HEAD_DIM + hd * XA_HEAD_DIM,
                   XA_HEADS * XA_HEAD_DIM + (hd + 1) * XA_HEAD_DIM)
        s = _dot_nt(xq_ref[:, cs], mkv_ref[0, :, cs]) * (XA_HEAD_DIM ** -0.5)
        s = s - jnp.max(s, axis=-1, keepdims=True)
        p = jnp.exp(s)
        p = p / jnp.sum(p, axis=-1, keepdims=True)
        oxa_ref[:, cs] = _dot(p.astype(BF16), mkv_ref[0, :, vs]).astype(BF16)

    d = D_MODEL
    merged = (gate_ref[:, 0:d].astype(F32) * _dot(oda_ref[...], wa_ref[...])
              + gate_ref[:, d:2 * d].astype(F32) * _dot(osg_ref[...], wsg_ref[...])
              + gate_ref[:, 2 * d:3 * d].astype(F32) * _dot(oxa_ref[...], wm_ref[...]))
    y = _dot(merged.astype(BF16), wo_ref[...])
    o_ref[...] = _layer_norm(ALPHA * x_ref[...] + y, g1_ref[...], b1_ref[...])


def _mix(z2, oda2, x2, mkv3, w_s, b_s_t, wa, wsg, wm, wo, ln_g, ln_b, tm, seq):
    n = x2.shape[0]
    d = D_MODEL
    full = lambda shape: pl.BlockSpec(shape, lambda i: (0,) * len(shape))
    return pl.pallas_call(
        functools.partial(_mix_kernel, tm=tm),
        grid=(n // tm,),
        in_specs=[
            pl.BlockSpec((tm, d), lambda i: (i, 3)),
            pl.BlockSpec((tm, d), lambda i: (i, 4)),
            pl.BlockSpec((tm, d), lambda i: (i, 5)),
            pl.BlockSpec((tm, 3 * d), lambda i: (i, 2)),
            pl.BlockSpec((tm, d), lambda i: (i, 0)),
            pl.BlockSpec((tm, d), lambda i: (i, 0)),
            pl.BlockSpec((1, mkv3.shape[1], mkv3.shape[2]), lambda i: ((i * tm) // seq, 0, 0)),
            full(w_s.shape), full(b_s_t.shape),
            full((d, d)), full((d, d)), full((d, d)), full((d, d)),
            full((1, d)), full((1, d)),
        ],
        out_specs=pl.BlockSpec((tm, d), lambda i: (i, 0)),
        out_shape=jax.ShapeDtypeStruct((n, d), F32),
        scratch_shapes=[pltpu.VMEM((tm, d), BF16), pltpu.VMEM((tm, d), BF16)],
        compiler_params=pltpu.CompilerParams(
            dimension_semantics=("arbitrary",), vmem_limit_bytes=VMEM_LIMIT),
        name="mix",
    )(z2, z2, z2, z2, oda2, x2, mkv3, w_s, b_s_t, wa, wsg, wm, wo, ln_g, ln_b)


def _ffn_kernel(x_ref, wi_ref, wo_ref, g_ref, b_ref, o_ref, *, d_ff):
    x = x_ref[...]
    h = _dot(x.astype(BF16), wi_ref[...])
    a = h[:, :d_ff]
    act = (a * _sigmoid(a)) * h[:, d_ff:]
    y = _dot(act.astype(BF16), wo_ref[...])
    o_ref[...] = _layer_norm(ALPHA * x + y, g_ref[...], b_ref[...])


def _ffn(x1, wi, wo, ln_g, ln_b, tm):
    n, d = x1.shape
    d_ff = wo.shape[0]
    full = lambda shape: pl.BlockSpec(shape, lambda i: (0,) * len(shape))
    return pl.pallas_call(
        functools.partial(_ffn_kernel, d_ff=d_ff),
        grid=(n // tm,),
        in_specs=[pl.BlockSpec((tm, d), lambda i: (i, 0)),
                  full(wi.shape), full(wo.shape), full((1, d)), full((1, d))],
        out_specs=pl.BlockSpec((tm, d), lambda i: (i, 0)),
        out_shape=jax.ShapeDtypeStruct((n, d), F32),
        compiler_params=pltpu.CompilerParams(
            dimension_semantics=("arbitrary",), vmem_limit_bytes=VMEM_LIMIT),
        name="ffn",
    )(x1, wi, wo, ln_g, ln_b)


def kernel(x, mem, positions, w_in, lambda_q1, lambda_k1, lambda_q2, lambda_k2, da_subln_g,
           sg_norm_g, sg_norm_b, sg_w_s, sg_b_s, w_mem_kv, w_br_attn, w_br_sg, w_br_mem, w_out,
           ln1_g, ln1_b, w_ffn_in, w_ffn_out, ln2_g, ln2_b):
    bsz, seq, d = x.shape
    n = bsz * seq
    depth = w_in.shape[0]
    assert d == D_MODEL and depth == DEPTH

    inv_freq = ROPE_THETA ** (-jnp.arange(ROT_HALF, dtype=F32) * 2.0 / ROT_DIM)
    lane_d = jnp.arange(LANES) % DA_HEAD_DIM
    invf = jnp.where(lane_d < ROT_DIM, inv_freq[lane_d % ROT_HALF], 0.0).reshape(1, LANES)
    pos2 = positions.reshape(n, 1)

    x2 = x.reshape(n, d)
    for l in range(depth):
        lambda_init = 0.8 - 0.6 * math.exp(-0.3 * l)
        row = lambda a: a[l].reshape(1, -1).astype(F32)
        z2 = _proj_in(x2, w_in[l].astype(BF16), pos2, invf, row(sg_norm_g), row(sg_norm_b), tm=1024)
        oda = _diff_attn(z2.reshape(bsz, seq, -1), row(lambda_q1), row(lambda_k1),
                         row(lambda_q2), row(lambda_k2), row(da_subln_g), tq=512,
                         lambda_init=lambda_init)
        mkv = _mem_kv(mem.reshape(-1, d), w_mem_kv[l].astype(BF16)).reshape(bsz, mem.shape[1], -1)
        x1 = _mix(z2, oda.reshape(n, -1), x2, mkv, sg_w_s[l], sg_b_s[l].T,
                  w_br_attn[l].astype(BF16), w_br_sg[l].astype(BF16), w_br_mem[l].astype(BF16),
                  w_out[l].astype(BF16), row(ln1_g), row(ln1_b), tm=256, seq=seq)
        x2 = _ffn(x1, w_ffn_in[l].astype(BF16), w_ffn_out[l].astype(BF16), row(ln2_g), row(ln2_b),
                  tm=256)
    return x2.reshape(bsz, seq, d)
```

```python
import functools
import math

import jax
import jax.numpy as jnp
from jax import lax
from jax.experimental import pallas as pl
from jax.experimental.pallas import tpu as pltpu

D_MODEL = 1024
DA_HEADS = 8
DA_HEAD_DIM = 64
DA_V_DIM = 2 * DA_HEAD_DIM
ROPE_THETA = 500000.0
ROT_DIM = DA_HEAD_DIM // 4
ROT_HALF = ROT_DIM // 2
SG_GROUPS = 8
SG_CHUNK = 128
SG_GROUP_DIM = 128
XA_HEADS = 4
XA_HEAD_DIM = 256
N_SECTIONS = 9
DEPTH = 1
ALPHA = (2 * DEPTH) ** 0.25
LN_EPS = 1e-5
RMS_EPS = 1e-5
LANES = 128
COL_CHUNK = 256
MIX_HALF = 256

VMEM_LIMIT = 56 * 1024 * 1024

F32 = jnp.float32
BF16 = jnp.bfloat16


def _dot(a, b):
    return jnp.dot(a, b, preferred_element_type=F32)


def _dot_nt(a, b):
    return lax.dot_general(a, b, (((1,), (1,)), ((), ())), preferred_element_type=F32)


def _sigmoid(x):
    return 0.5 + 0.5 * jnp.tanh(0.5 * x)


def _layer_norm(x, g, b):
    mu = jnp.mean(x, axis=-1, keepdims=True)
    xc = x - mu
    var = jnp.mean(xc * xc, axis=-1, keepdims=True)
    return xc * lax.rsqrt(var + LN_EPS) * g + b


def _proj_in_kernel(x_ref, w_ref, pos_ref, invf_ref, sgg_ref, sgb_ref, o_ref,
                    xb_ref, cos_ref, sin_ref, gel_ref):
    j = pl.program_id(1)

    @pl.when(j == 0)
    def _():
        xb_ref[...] = x_ref[...].astype(BF16)
        ang = pos_ref[...].astype(F32) * invf_ref[...]
        cos_ref[...] = jnp.cos(ang)
        sin_ref[...] = jnp.sin(ang)

    def run(epilogue):
        for c in range(D_MODEL // COL_CHUNK):
            cs = slice(c * COL_CHUNK, (c + 1) * COL_CHUNK)
            epilogue(_dot(xb_ref[...], w_ref[:, cs]), cs)

    def rope(scale):
        d = lax.broadcasted_iota(jnp.int32, (1, LANES), 1) % DA_HEAD_DIM
        sgn_lo = jnp.where(d < ROT_HALF, -scale, 0.0)
        sgn_hi = jnp.where((d >= ROT_HALF) & (d < ROT_DIM), scale, 0.0)

        def epilogue(acc, cs):
            cos = cos_ref[...] * scale
            sin_lo = sin_ref[...] * sgn_lo
            sin_hi = sin_ref[...] * sgn_hi
            for h in range(COL_CHUNK // LANES):
                t = acc[:, h * LANES:(h + 1) * LANES]
                t_up = pltpu.roll(t, LANES - ROT_HALF, 1)
                t_dn = pltpu.roll(t, ROT_HALF, 1)
                r = t * cos + t_up * sin_lo + t_dn * sin_hi
                o_ref[:, cs.start + h * LANES:cs.start + (h + 1) * LANES] = r.astype(BF16)
        return epilogue

    @pl.when(j == 0)
    def _():
        run(rope(DA_HEAD_DIM ** -0.5 * math.log2(math.e)))

    @pl.when(j == 1)
    def _():
        run(rope(1.0))

    @pl.when((j == 2) | (j == 5))
    def _():
        def epilogue(acc, cs):
            o_ref[:, cs] = acc.astype(BF16)
        run(epilogue)

    @pl.when(j == 3)
    def _():
        def epilogue(acc, cs):
            o_ref[:, cs] = jax.nn.gelu(acc).astype(BF16)
        run(epilogue)

    @pl.when(j == 4)
    def _():
        def epilogue(acc, cs):
            gel_ref[:, cs] = jax.nn.gelu(acc)
        run(epilogue)
        o_ref[...] = _layer_norm(gel_ref[...], sgg_ref[...], sgb_ref[...]).astype(BF16)

    @pl.when(j >= 6)
    def _():
        def epilogue(acc, cs):
            o_ref[:, cs] = _sigmoid(acc).astype(BF16)
        run(epilogue)


def _proj_in(x2, w_in_b, pos2, invf, sgg, sgb, tm):
    n = x2.shape[0]
    return pl.pallas_call(
        _proj_in_kernel,
        grid=(n // tm, N_SECTIONS),
        in_specs=[
            pl.BlockSpec((tm, D_MODEL), lambda i, j: (i, 0)),
            pl.BlockSpec((D_MODEL, D_MODEL), lambda i, j: (0, j)),
            pl.BlockSpec((tm, 1), lambda i, j: (i, 0)),
            pl.BlockSpec((1, LANES), lambda i, j: (0, 0)),
            pl.BlockSpec((1, D_MODEL), lambda i, j: (0, 0)),
            pl.BlockSpec((1, D_MODEL), lambda i, j: (0, 0)),
        ],
        out_specs=pl.BlockSpec((tm, D_MODEL), lambda i, j: (i, j)),
        out_shape=jax.ShapeDtypeStruct((n, N_SECTIONS * D_MODEL), BF16),
        scratch_shapes=[
            pltpu.VMEM((tm, D_MODEL), BF16),
            pltpu.VMEM((tm, LANES), F32),
            pltpu.VMEM((tm, LANES), F32),
            pltpu.VMEM((tm, D_MODEL), F32),
        ],
        compiler_params=pltpu.CompilerParams(
            dimension_semantics=("arbitrary", "arbitrary"), vmem_limit_bytes=VMEM_LIMIT),
        name="proj_in",
    )(x2, w_in_b, pos2, invf, sgg, sgb)


def _diff_attn_kernel(q_ref, k_ref, v_ref, lq1_ref, lk1_ref, lq2_ref, lk2_ref, g_ref, o_ref,
                      qs_ref, s0_ref, s1_ref, mx0_ref, mx1_ref, m_ref, l_ref, acc_ref,
                      *, tq, tk, lambda_init):
    s_ref = (s0_ref, s1_ref)
    mx_ref = (mx0_ref, mx1_ref)
    i = pl.program_id(2)
    q = q_ref[0]
    lane = lax.broadcasted_iota(jnp.int32, (1, LANES), 1)
    qs_ref[0:tq, :] = jnp.where(lane < DA_HEAD_DIM, q, jnp.zeros_like(q))
    qs_ref[tq:2 * tq, :] = jnp.where(lane >= DA_HEAD_DIM, q, jnp.zeros_like(q))

    rows = 2 * tq
    m_ref[...] = jnp.full(m_ref.shape, -jnp.inf, F32)
    l_ref[...] = jnp.zeros(l_ref.shape, F32)
    acc_ref[...] = jnp.zeros(acc_ref.shape, F32)

    def scores(jb):
        start = pl.multiple_of(jb * tk, tk)
        kb = k_ref[0, pl.ds(start, tk), :]
        s = _dot_nt(qs_ref[...], kb)
        smax = s[:, 0:LANES]
        for g in range(1, tk // LANES):
            smax = jnp.maximum(smax, s[:, g * LANES:(g + 1) * LANES])
        return s, smax

    def softmax_pv(jb, slot, width, mask_from, smax=None):
        ngrp = width // LANES
        start = pl.multiple_of(jb * tk, tk)
        vb = v_ref[0, pl.ds(start, width), :]
        row = lax.broadcasted_iota(jnp.int32, (rows, LANES), 0) & (tq - 1)
        col = lax.broadcasted_iota(jnp.int32, (rows, LANES), 1)

        def sgrp(g):
            x = s_ref[slot][:, g * LANES:(g + 1) * LANES]
            if g >= mask_from:
                x = jnp.where(col + (g - mask_from) * LANES <= row, x, -jnp.inf)
            return x

        if smax is None:
            smax = sgrp(0)
            for g in range(1, ngrp):
                smax = jnp.maximum(smax, sgrp(g))
        m_old = m_ref[...]
        m_new = jnp.maximum(m_old, jnp.max(smax, axis=-1, keepdims=True))
        m_ref[...] = m_new
        a = jnp.exp2(m_old - m_new)
        p = jnp.concatenate([jnp.exp2((sgrp(g) - m_new).astype(BF16)) for g in range(ngrp)],
                            axis=1)
        v1 = jnp.concatenate([vb, jnp.ones((width, LANES), BF16)], axis=1)
        pv = _dot(p, v1)
        acc_ref[...] = a * acc_ref[...] + pv[:, :DA_V_DIM]
        l_ref[...] = a * l_ref[...] + pv[:, DA_V_DIM:]

    nfull = lax.shift_right_logical(i, 1)
    npair = lax.shift_right_logical(nfull, 1)
    s_ref[0][...], mx_ref[0][...] = scores(0)

    def step(jb, slot):
        s_ref[1 - slot][...], mx_ref[1 - slot][...] = scores(jb + 1)
        softmax_pv(jb, slot, tk, tk // LANES, smax=mx_ref[slot][...])

    def body(pair, carry):
        step(2 * pair, 0)
        step(2 * pair + 1, 1)
        return carry

    lax.fori_loop(0, npair, body, 0)

    @pl.when((nfull & 1) == 1)
    def _():
        step(nfull - 1, 0)

    for slot in range(2):
        @pl.when(((i & 1) == 0) & ((nfull & 1) == slot))
        def _():
            softmax_pv(nfull, slot, tq, 0)

        @pl.when(((i & 1) == 1) & ((nfull & 1) == slot))
        def _():
            softmax_pv(nfull, slot, tk, tq // LANES)

    lam = (jnp.exp(jnp.sum(lq1_ref[...] * lk1_ref[...], axis=-1, keepdims=True))
           - jnp.exp(jnp.sum(lq2_ref[...] * lk2_ref[...], axis=-1, keepdims=True))
           + lambda_init)
    o = (acc_ref[0:tq, :] / l_ref[0:tq, :]
         - lam * (acc_ref[tq:2 * tq, :] / l_ref[tq:2 * tq, :]))
    ms = jnp.mean(o * o, axis=-1, keepdims=True)
    o = o * lax.rsqrt(ms + RMS_EPS) * g_ref[...] * (1.0 - lambda_init)
    o_ref[0] = o.astype(BF16)


def _diff_attn(z3, lq1, lk1, lq2, lk2, subln_g, tq, lambda_init):
    bsz, seq, _ = z3.shape
    tk = 2 * tq
    kern = functools.partial(_diff_attn_kernel, tq=tq, tk=tk, lambda_init=lambda_init)
    vec = lambda w: pl.BlockSpec((1, w), lambda b, h, i: (0, 0))
    return pl.pallas_call(
        kern,
        grid=(bsz, DA_HEADS, seq // tq),
        in_specs=[
            pl.BlockSpec((1, tq, LANES), lambda b, h, i: (b, i, h)),
            pl.BlockSpec((1, seq, LANES), lambda b, h, i: (b, 0, DA_HEADS + h)),
            pl.BlockSpec((1, seq, LANES), lambda b, h, i: (b, 0, 2 * DA_HEADS + h)),
            vec(DA_HEAD_DIM), vec(DA_HEAD_DIM), vec(DA_HEAD_DIM), vec(DA_HEAD_DIM),
            vec(DA_V_DIM),
        ],
        out_specs=pl.BlockSpec((1, tq, LANES), lambda b, h, i: (b, i, h)),
        out_shape=jax.ShapeDtypeStruct((bsz, seq, DA_HEADS * DA_V_DIM), BF16),
        scratch_shapes=[
            pltpu.VMEM((2 * tq, LANES), BF16),
            pltpu.VMEM((2 * tq, tk), F32),
            pltpu.VMEM((2 * tq, tk), F32),
            pltpu.VMEM((2 * tq, LANES), F32),
            pltpu.VMEM((2 * tq, LANES), F32),
            pltpu.VMEM((2 * tq, LANES), F32),
            pltpu.VMEM((2 * tq, LANES), F32),
            pltpu.VMEM((2 * tq, DA_V_DIM), F32),
        ],
        compiler_params=pltpu.CompilerParams(
            dimension_semantics=("arbitrary", "arbitrary", "arbitrary"),
            vmem_limit_bytes=VMEM_LIMIT),
        name="diff_attn",
    )(z3, z3, z3, lq1, lk1, lq2, lk2, subln_g)


def _mem_kv_kernel(m_ref, w_ref, o_ref):
    o_ref[...] = _dot(m_ref[...].astype(BF16), w_ref[...]).astype(BF16)


def _mem_kv(mem2, w_b):
    rows, width = mem2.shape[0], w_b.shape[1]
    tn = 1024
    return pl.pallas_call(
        _mem_kv_kernel,
        grid=(width // tn,),
        in_specs=[pl.BlockSpec((rows, D_MODEL), lambda j: (0, 0)),
                  pl.BlockSpec((D_MODEL, tn), lambda j: (0, j))],
        out_specs=pl.BlockSpec((rows, tn), lambda j: (0, j)),
        out_shape=jax.ShapeDtypeStruct((rows, width), BF16),
        compiler_params=pltpu.CompilerParams(
            dimension_semantics=("arbitrary",), vmem_limit_bytes=VMEM_LIMIT),
        name="mem_kv",
    )(mem2, w_b)


def _mix_kernel(u_ref, v_ref, xq_ref, gate_ref, oda_ref, x_ref, mkv_ref, ws_ref, bs_ref,
                wa_ref, wsg_ref, wm_ref, wo_ref, g1_ref, b1_ref, o_ref, osg_ref, oxa_ref, *, tm):
    halves = [slice(h * MIX_HALF, (h + 1) * MIX_HALF) for h in range(tm // MIX_HALF)]
    d = D_MODEL
    row = lax.broadcasted_iota(jnp.int32, (SG_CHUNK, SG_CHUNK), 0)
    col = lax.broadcasted_iota(jnp.int32, (SG_CHUNK, SG_CHUNK), 1)
    ws = [jnp.where(col <= row, ws_ref[g], 0.0).astype(BF16) for g in range(SG_GROUPS)]

    def chunks(hs):
        return [slice(r, r + SG_CHUNK) for r in range(hs.start, hs.stop, SG_CHUNK)]

    def xa_cols(hd):
        return slice(hd * XA_HEAD_DIM, (hd + 1) * XA_HEAD_DIM)

    sg = {}
    for hs in halves:
        for g in range(SG_GROUPS):
            cs = slice(g * SG_GROUP_DIM, (g + 1) * SG_GROUP_DIM)
            for rs in chunks(hs):
                sg[rs.start, g] = _dot(ws[g], v_ref[rs, cs]) + bs_ref[:, g:g + 1]
    t_attn = [_dot(oda_ref[hs, :], wa_ref[...]) for hs in halves]
    xs = [[_dot_nt(xq_ref[hs, xa_cols(hd)], mkv_ref[0, :, xa_cols(hd)]) * (XA_HEAD_DIM ** -0.5)
           for hd in range(XA_HEADS)] for hs in halves]
    for hs in halves:
        for g in range(SG_GROUPS):
            cs = slice(g * SG_GROUP_DIM, (g + 1) * SG_GROUP_DIM)
            for rs in chunks(hs):
                osg_ref[rs, cs] = (u_ref[rs, cs].astype(F32) * sg[rs.start, g]).astype(BF16)
    t_sg = [_dot(osg_ref[hs, :], wsg_ref[...]) for hs in halves]
    for hi, hs in enumerate(halves):
        for hd in range(XA_HEADS):
            s = xs[hi][hd]
            p = jnp.exp(s - jnp.max(s, axis=-1, keepdims=True))
            p = p / jnp.sum(p, axis=-1, keepdims=True)
            vs = slice(XA_HEADS * XA_HEAD_DIM + hd * XA_HEAD_DIM,
                       XA_HEADS * XA_HEAD_DIM + (hd + 1) * XA_HEAD_DIM)
            oxa_ref[hs, xa_cols(hd)] = _dot(p.astype(BF16), mkv_ref[0, :, vs]).astype(BF16)
    t_mem = [_dot(oxa_ref[hs, :], wm_ref[...]) for hs in halves]
    merged = [(gate_ref[hs, 0:d].astype(F32) * t_attn[hi]
               + gate_ref[hs, d:2 * d].astype(F32) * t_sg[hi]
               + gate_ref[hs, 2 * d:3 * d].astype(F32) * t_mem[hi]).astype(BF16)
              for hi, hs in enumerate(halves)]
    y = [_dot(m, wo_ref[...]) for m in merged]
    for hi, hs in enumerate(halves):
        o_ref[hs, :] = _layer_norm(ALPHA * x_ref[hs, :] + y[hi], g1_ref[...], b1_ref[...])


def _mix(z2, oda2, x2, mkv3, w_s, b_s_t, wa, wsg, wm, wo, ln_g, ln_b, tm, seq):
    n = x2.shape[0]
    d = D_MODEL
    full = lambda shape: pl.BlockSpec(shape, lambda i: (0,) * len(shape))
    return pl.pallas_call(
        functools.partial(_mix_kernel, tm=tm),
        grid=(n // tm,),
        in_specs=[
            pl.BlockSpec((tm, d), lambda i: (i, 3)),
            pl.BlockSpec((tm, d), lambda i: (i, 4)),
            pl.BlockSpec((tm, d), lambda i: (i, 5)),
            pl.BlockSpec((tm, 3 * d), lambda i: (i, 2)),
            pl.BlockSpec((tm, d), lambda i: (i, 0)),
            pl.BlockSpec((tm, d), lambda i: (i, 0)),
            pl.BlockSpec((1, mkv3.shape[1], mkv3.shape[2]), lambda i: ((i * tm) // seq, 0, 0)),
            full(w_s.shape), full(b_s_t.shape),
            full((d, d)), full((d, d)), full((d, d)), full((d, d)),
            full((1, d)), full((1, d)),
        ],
        out_specs=pl.BlockSpec((tm, d), lambda i: (i, 0)),
        out_shape=jax.ShapeDtypeStruct((n, d), F32),
        scratch_shapes=[pltpu.VMEM((tm, d), BF16), pltpu.VMEM((tm, d), BF16)],
        compiler_params=pltpu.CompilerParams(
            dimension_semantics=("arbitrary",), vmem_limit_bytes=VMEM_LIMIT),
        name="mix",
    )(z2, z2, z2, z2, oda2, x2, mkv3, w_s, b_s_t, wa, wsg, wm, wo, ln_g, ln_b)


def _ffn_kernel(x_ref, wi_ref, wo_ref, g_ref, b_ref, o_ref, *, d_ff):
    x = x_ref[...]
    h = _dot(x.astype(BF16), wi_ref[...])
    a = h[:, :d_ff]
    act = (a * _sigmoid(a)) * h[:, d_ff:]
    y = _dot(act.astype(BF16), wo_ref[...])
    o_ref[...] = _layer_norm(ALPHA * x + y, g_ref[...], b_ref[...])


def _ffn(x1, wi, wo, ln_g, ln_b, tm):
    n, d = x1.shape
    d_ff = wo.shape[0]
    full = lambda shape: pl.BlockSpec(shape, lambda i: (0,) * len(shape))
    return pl.pallas_call(
        functools.partial(_ffn_kernel, d_ff=d_ff),
        grid=(n // tm,),
        in_specs=[pl.BlockSpec((tm, d), lambda i: (i, 0)),
                  full(wi.shape), full(wo.shape), full((1, d)), full((1, d))],
        out_specs=pl.BlockSpec((tm, d), lambda i: (i, 0)),
        out_shape=jax.ShapeDtypeStruct((n, d), F32),
        compiler_params=pltpu.CompilerParams(
            dimension_semantics=("arbitrary",), vmem_limit_bytes=VMEM_LIMIT),
        name="ffn",
    )(x1, wi, wo, ln_g, ln_b)


def kernel(x, mem, positions, w_in, lambda_q1, lambda_k1, lambda_q2, lambda_k2, da_subln_g,
           sg_norm_g, sg_norm_b, sg_w_s, sg_b_s, w_mem_kv, w_br_attn, w_br_sg, w_br_mem, w_out,
           ln1_g, ln1_b, w_ffn_in, w_ffn_out, ln2_g, ln2_b):
    bsz, seq, d = x.shape
    n = bsz * seq
    depth = w_in.shape[0]
    assert d == D_MODEL and depth == DEPTH

    inv_freq = ROPE_THETA ** (-jnp.arange(ROT_HALF, dtype=F32) * 2.0 / ROT_DIM)
    lane_d = jnp.arange(LANES) % DA_HEAD_DIM
    invf = jnp.where(lane_d < ROT_DIM, inv_freq[lane_d % ROT_HALF], 0.0).reshape(1, LANES)
    pos2 = positions.reshape(n, 1)

    x2 = x.reshape(n, d)
    for l in range(depth):
        lambda_init = 0.8 - 0.6 * math.exp(-0.3 * l)
        row = lambda a: a[l].reshape(1, -1).astype(F32)
        z2 = _proj_in(x2, w_in[l].astype(BF16), pos2, invf, row(sg_norm_g), row(sg_norm_b), tm=1024)
        oda = _diff_attn(z2.reshape(bsz, seq, -1), row(lambda_q1), row(lambda_k1),
                         row(lambda_q2), row(lambda_k2), row(da_subln_g), tq=512,
                         lambda_init=lambda_init)
        mkv = _mem_kv(mem.reshape(-1, d), w_mem_kv[l].astype(BF16)).reshape(bsz, mem.shape[1], -1)
        x1 = _mix(z2, oda.reshape(n, -1), x2, mkv, sg_w_s[l], sg_b_s[l].T,
                  w_br_attn[l].astype(BF16), w_br_sg[l].astype(BF16), w_br_mem[l].astype(BF16),
                  w_out[l].astype(BF16), row(ln1_g), row(ln1_b), tm=512, seq=seq)
        x2 = _ffn(x1, w_ffn_in[l].astype(BF16), w_ffn_out[l].astype(BF16), row(ln2_g), row(ln2_b),
                  tm=256)
    return x2.reshape(bsz, seq, d)
```

```python
import functools
import math

import jax
import jax.numpy as jnp
from jax import lax
from jax.experimental import pallas as pl
from jax.experimental.pallas import tpu as pltpu

D_MODEL = 1024
DA_HEADS = 8
DA_HEAD_DIM = 64
DA_V_DIM = 2 * DA_HEAD_DIM
ROPE_THETA = 500000.0
ROT_DIM = DA_HEAD_DIM // 4
ROT_HALF = ROT_DIM // 2
SG_GROUPS = 8
SG_CHUNK = 128
SG_GROUP_DIM = 128
XA_HEADS = 4
XA_HEAD_DIM = 256
N_SECTIONS = 9
SECTIONS_PER_GROUP = 3
DEPTH = 1
ALPHA = (2 * DEPTH) ** 0.25
LN_EPS = 1e-5
RMS_EPS = 1e-5
LANES = 128
COL_CHUNK = 256
MIX_HALF = 256

VMEM_LIMIT = 56 * 1024 * 1024

F32 = jnp.float32
BF16 = jnp.bfloat16


def _dot(a, b):
    return jnp.dot(a, b, preferred_element_type=F32)


def _dot_nt(a, b):
    return lax.dot_general(a, b, (((1,), (1,)), ((), ())), preferred_element_type=F32)


def _sigmoid(x):
    return 0.5 + 0.5 * jnp.tanh(0.5 * x)


def _layer_norm(x, g, b):
    mu = jnp.mean(x, axis=-1, keepdims=True)
    xc = x - mu
    var = jnp.mean(xc * xc, axis=-1, keepdims=True)
    return xc * lax.rsqrt(var + LN_EPS) * g + b


def _proj_in_kernel(x_ref, w_ref, pos_ref, invf_ref, sgg_ref, sgb_ref, o_ref,
                    xb_ref, cos_ref, sin_ref, gel_ref):
    grp = pl.program_id(1)

    def run(sec, epilogue):
        for c in range(D_MODEL // COL_CHUNK):
            cs = slice(sec * D_MODEL + c * COL_CHUNK, sec * D_MODEL + (c + 1) * COL_CHUNK)
            epilogue(_dot(xb_ref[...], w_ref[:, cs]), cs)

    def plain(acc, cs):
        o_ref[:, cs] = acc.astype(BF16)

    def rope(scale):
        d = lax.broadcasted_iota(jnp.int32, (1, LANES), 1) % DA_HEAD_DIM
        sgn_lo = jnp.where(d < ROT_HALF, -scale, 0.0)
        sgn_hi = jnp.where((d >= ROT_HALF) & (d < ROT_DIM), scale, 0.0)

        def epilogue(acc, cs):
            cos = cos_ref[...] * scale
            sin_lo = sin_ref[...] * sgn_lo
            sin_hi = sin_ref[...] * sgn_hi
            for h in range(COL_CHUNK // LANES):
                t = acc[:, h * LANES:(h + 1) * LANES]
                t_up = pltpu.roll(t, LANES - ROT_HALF, 1)
                t_dn = pltpu.roll(t, ROT_HALF, 1)
                r = t * cos + t_up * sin_lo + t_dn * sin_hi
                o_ref[:, cs.start + h * LANES:cs.start + (h + 1) * LANES] = r.astype(BF16)
        return epilogue

    @pl.when(grp == 0)
    def _():
        xb_ref[...] = x_ref[...].astype(BF16)
        run(2, plain)
        ang = pos_ref[...].astype(F32) * invf_ref[...]
        cos_ref[...] = jnp.cos(ang)
        sin_ref[...] = jnp.sin(ang)
        run(0, rope(DA_HEAD_DIM ** -0.5 * math.log2(math.e)))
        run(1, rope(1.0))

    @pl.when(grp == 1)
    def _():
        def gelu_keep(acc, cs):
            gel_ref[:, cs.start - D_MODEL:cs.stop - D_MODEL] = jax.nn.gelu(acc)

        def gelu_out(acc, cs):
            o_ref[:, cs] = jax.nn.gelu(acc).astype(BF16)

        run(1, gelu_keep)
        run(0, gelu_out)
        o_ref[:, D_MODEL:2 * D_MODEL] = _layer_norm(
            gel_ref[...], sgg_ref[...], sgb_ref[...]).astype(BF16)
        run(2, plain)

    @pl.when(grp == 2)
    def _():
        def gate(acc, cs):
            o_ref[:, cs] = _sigmoid(acc).astype(BF16)
        for sec in range(SECTIONS_PER_GROUP):
            run(sec, gate)


def _proj_in(x2, w_in_b, pos2, invf, sgg, sgb, tm):
    n = x2.shape[0]
    gw = SECTIONS_PER_GROUP * D_MODEL
    return pl.pallas_call(
        _proj_in_kernel,
        grid=(n // tm, N_SECTIONS // SECTIONS_PER_GROUP),
        in_specs=[
            pl.BlockSpec((tm, D_MODEL), lambda i, j: (i, 0)),
            pl.BlockSpec((D_MODEL, gw), lambda i, j: (0, j)),
            pl.BlockSpec((tm, 1), lambda i, j: (i, 0)),
            pl.BlockSpec((1, LANES), lambda i, j: (0, 0)),
            pl.BlockSpec((1, D_MODEL), lambda i, j: (0, 0)),
            pl.BlockSpec((1, D_MODEL), lambda i, j: (0, 0)),
        ],
        out_specs=pl.BlockSpec((tm, gw), lambda i, j: (i, j)),
        out_shape=jax.ShapeDtypeStruct((n, N_SECTIONS * D_MODEL), BF16),
        scratch_shapes=[
            pltpu.VMEM((tm, D_MODEL), BF16),
            pltpu.VMEM((tm, LANES), F32),
            pltpu.VMEM((tm, LANES), F32),
            pltpu.VMEM((tm, D_MODEL), F32),
        ],
        compiler_params=pltpu.CompilerParams(
            dimension_semantics=("arbitrary", "arbitrary"), vmem_limit_bytes=VMEM_LIMIT),
        name="proj_in",
    )(x2, w_in_b, pos2, invf, sgg, sgb)


def _diff_attn_kernel(q_ref, k_ref, v_ref, lq1_ref, lk1_ref, lq2_ref, lk2_ref, g_ref, o_ref,
                      qs_ref, s0_ref, s1_ref, mx0_ref, mx1_ref, m_ref, l_ref, acc_ref,
                      *, tq, tk, lambda_init):
    s_ref = (s0_ref, s1_ref)
    mx_ref = (mx0_ref, mx1_ref)
    i = pl.program_id(2)
    q = q_ref[0]
    lane = lax.broadcasted_iota(jnp.int32, (1, LANES), 1)
    qs_ref[0:tq, :] = jnp.where(lane < DA_HEAD_DIM, q, jnp.zeros_like(q))
    qs_ref[tq:2 * tq, :] = jnp.where(lane >= DA_HEAD_DIM, q, jnp.zeros_like(q))

    rows = 2 * tq
    m_ref[...] = jnp.full(m_ref.shape, -jnp.inf, F32)
    l_ref[...] = jnp.zeros(l_ref.shape, F32)
    acc_ref[...] = jnp.zeros(acc_ref.shape, F32)

    def scores(jb):
        start = pl.multiple_of(jb * tk, tk)
        kb = k_ref[0, pl.ds(start, tk), :]
        s = _dot_nt(qs_ref[...], kb)
        smax = s[:, 0:LANES]
        for g in range(1, tk // LANES):
            smax = jnp.maximum(smax, s[:, g * LANES:(g + 1) * LANES])
        return s, smax

    def softmax_pv(jb, slot, width, mask_from, smax=None):
        ngrp = width // LANES
        start = pl.multiple_of(jb * tk, tk)
        vb = v_ref[0, pl.ds(start, width), :]
        row = lax.broadcasted_iota(jnp.int32, (rows, LANES), 0) & (tq - 1)
        col = lax.broadcasted_iota(jnp.int32, (rows, LANES), 1)

        def sgrp(g):
            x = s_ref[slot][:, g * LANES:(g + 1) * LANES]
            if g >= mask_from:
                x = jnp.where(col + (g - mask_from) * LANES <= row, x, -jnp.inf)
            return x

        if smax is None:
            smax = sgrp(0)
            for g in range(1, ngrp):
                smax = jnp.maximum(smax, sgrp(g))
        m_old = m_ref[...]
        m_new = jnp.maximum(m_old, jnp.max(smax, axis=-1, keepdims=True))
        m_ref[...] = m_new
        a = jnp.exp2(m_old - m_new)
        p = jnp.concatenate([jnp.exp2((sgrp(g) - m_new).astype(BF16)) for g in range(ngrp)],
                            axis=1)
        v1 = jnp.concatenate([vb, jnp.ones((width, LANES), BF16)], axis=1)
        pv = _dot(p, v1)
        acc_ref[...] = a * acc_ref[...] + pv[:, :DA_V_DIM]
        l_ref[...] = a * l_ref[...] + pv[:, DA_V_DIM:]

    nfull = lax.shift_right_logical(i, 1)
    npair = lax.shift_right_logical(nfull, 1)
    s_ref[0][...], mx_ref[0][...] = scores(0)

    def step(jb, slot):
        s_ref[1 - slot][...], mx_ref[1 - slot][...] = scores(jb + 1)
        softmax_pv(jb, slot, tk, tk // LANES, smax=mx_ref[slot][...])

    def body(pair, carry):
        step(2 * pair, 0)
        step(2 * pair + 1, 1)
        return carry

    lax.fori_loop(0, npair, body, 0)

    @pl.when((nfull & 1) == 1)
    def _():
        step(nfull - 1, 0)

    for slot in range(2):
        @pl.when(((i & 1) == 0) & ((nfull & 1) == slot))
        def _():
            softmax_pv(nfull, slot, tq, 0)

        @pl.when(((i & 1) == 1) & ((nfull & 1) == slot))
        def _():
            softmax_pv(nfull, slot, tk, tq // LANES)

    lam = (jnp.exp(jnp.sum(lq1_ref[...] * lk1_ref[...], axis=-1, keepdims=True))
           - jnp.exp(jnp.sum(lq2_ref[...] * lk2_ref[...], axis=-1, keepdims=True))
           + lambda_init)
    o = (acc_ref[0:tq, :] / l_ref[0:tq, :]
         - lam * (acc_ref[tq:2 * tq, :] / l_ref[tq:2 * tq, :]))
    ms = jnp.mean(o * o, axis=-1, keepdims=True)
    o = o * lax.rsqrt(ms + RMS_EPS) * g_ref[...] * (1.0 - lambda_init)
    o_ref[0] = o.astype(BF16)


def _diff_attn(z3, lq1, lk1, lq2, lk2, subln_g, tq, lambda_init):
    bsz, seq, _ = z3.shape
    tk = 2 * tq
    kern = functools.partial(_diff_attn_kernel, tq=tq, tk=tk, lambda_init=lambda_init)
    vec = lambda w: pl.BlockSpec((1, w), lambda b, h, i: (0, 0))
    return pl.pallas_call(
        kern,
        grid=(bsz, DA_HEADS, seq // tq),
        in_specs=[
            pl.BlockSpec((1, tq, LANES), lambda b, h, i: (b, i, h)),
            pl.BlockSpec((1, seq, LANES), lambda b, h, i: (b, 0, DA_HEADS + h)),
            pl.BlockSpec((1, seq, LANES), lambda b, h, i: (b, 0, 2 * DA_HEADS + h)),
            vec(DA_HEAD_DIM), vec(DA_HEAD_DIM), vec(DA_HEAD_DIM), vec(DA_HEAD_DIM),
            vec(DA_V_DIM),
        ],
        out_specs=pl.BlockSpec((1, tq, LANES), lambda b, h, i: (b, i, h)),
        out_shape=jax.ShapeDtypeStruct((bsz, seq, DA_HEADS * DA_V_DIM), BF16),
        scratch_shapes=[
            pltpu.VMEM((2 * tq, LANES), BF16),
            pltpu.VMEM((2 * tq, tk), F32),
            pltpu.VMEM((2 * tq, tk), F32),
            pltpu.VMEM((2 * tq, LANES), F32),
            pltpu.VMEM((2 * tq, LANES), F32),
            pltpu.VMEM((2 * tq, LANES), F32),
            pltpu.VMEM((2 * tq, LANES), F32),
            pltpu.VMEM((2 * tq, DA_V_DIM), F32),
        ],
        compiler_params=pltpu.CompilerParams(
            dimension_semantics=("arbitrary", "arbitrary", "arbitrary"),
            vmem_limit_bytes=VMEM_LIMIT),
        name="diff_attn",
    )(z3, z3, z3, lq1, lk1, lq2, lk2, subln_g)


def _mem_kv_kernel(m_ref, w_ref, o_ref):
    o_ref[...] = _dot(m_ref[...].astype(BF16), w_ref[...]).astype(BF16)


def _mem_kv(mem2, w_b):
    rows, width = mem2.shape[0], w_b.shape[1]
    tn = 1024
    return pl.pallas_call(
        _mem_kv_kernel,
        grid=(width // tn,),
        in_specs=[pl.BlockSpec((rows, D_MODEL), lambda j: (0, 0)),
                  pl.BlockSpec((D_MODEL, tn), lambda j: (0, j))],
        out_specs=pl.BlockSpec((rows, tn), lambda j: (0, j)),
        out_shape=jax.ShapeDtypeStruct((rows, width), BF16),
        compiler_params=pltpu.CompilerParams(
            dimension_semantics=("arbitrary",), vmem_limit_bytes=VMEM_LIMIT),
        name="mem_kv",
    )(mem2, w_b)


def _mix_kernel(u_ref, v_ref, xq_ref, gate_ref, oda_ref, x_ref, mkv_ref, ws_ref, bs_ref,
                wa_ref, wsg_ref, wm_ref, wo_ref, g1_ref, b1_ref, o_ref, osg_ref, oxa_ref, *, tm):
    halves = [slice(h * MIX_HALF, (h + 1) * MIX_HALF) for h in range(tm // MIX_HALF)]
    d = D_MODEL
    row = lax.broadcasted_iota(jnp.int32, (SG_CHUNK, SG_CHUNK), 0)
    col = lax.broadcasted_iota(jnp.int32, (SG_CHUNK, SG_CHUNK), 1)
    ws = [jnp.where(col <= row, ws_ref[g], 0.0).astype(BF16) for g in range(SG_GROUPS)]

    def chunks(hs):
        return [slice(r, r + SG_CHUNK) for r in range(hs.start, hs.stop, SG_CHUNK)]

    def xa_cols(hd):
        return slice(hd * XA_HEAD_DIM, (hd + 1) * XA_HEAD_DIM)

    sg = {}
    for hs in halves:
        for g in range(SG_GROUPS):
            cs = slice(g * SG_GROUP_DIM, (g + 1) * SG_GROUP_DIM)
            for rs in chunks(hs):
                sg[rs.start, g] = _dot(ws[g], v_ref[rs, cs]) + bs_ref[:, g:g + 1]
    t_attn = [_dot(oda_ref[hs, :], wa_ref[...]) for hs in halves]
    xs = [[_dot_nt(xq_ref[hs, xa_cols(hd)], mkv_ref[0, :, xa_cols(hd)]) * (XA_HEAD_DIM ** -0.5)
           for hd in range(XA_HEADS)] for hs in halves]
    for hs in halves:
        for g in range(SG_GROUPS):
            cs = slice(g * SG_GROUP_DIM, (g + 1) * SG_GROUP_DIM)
            for rs in chunks(hs):
                osg_ref[rs, cs] = (u_ref[rs, cs].astype(F32) * sg[rs.start, g]).astype(BF16)
    t_sg = [_dot(osg_ref[hs, :], wsg_ref[...]) for hs in halves]
    for hi, hs in enumerate(halves):
        for hd in range(XA_HEADS):
            s = xs[hi][hd]
            p = jnp.exp(s - jnp.max(s, axis=-1, keepdims=True))
            p = p / jnp.sum(p, axis=-1, keepdims=True)
            vs = slice(XA_HEADS * XA_HEAD_DIM + hd * XA_HEAD_DIM,
                       XA_HEADS * XA_HEAD_DIM + (hd + 1) * XA_HEAD_DIM)
            oxa_ref[hs, xa_cols(hd)] = _dot(p.astype(BF16), mkv_ref[0, :, vs]).astype(BF16)
    t_mem = [_dot(oxa_ref[hs, :], wm_ref[...]) for hs in halves]
    merged = [(gate_ref[hs, 0:d].astype(F32) * t_attn[hi]
               + gate_ref[hs, d:2 * d].astype(F32) * t_sg[hi]
               + gate_ref[hs, 2 * d:3 * d].astype(F32) * t_mem[hi]).astype(BF16)
              for hi, hs in enumerate(halves)]
    y = [_dot(m, wo_ref[...]) for m in merged]
    for hi, hs in enumerate(halves):
        o_ref[hs, :] = _layer_norm(ALPHA * x_ref[hs, :] + y[hi], g1_ref[...], b1_ref[...])


def _mix(z2, oda2, x2, mkv3, w_s, b_s_t, wa, wsg, wm, wo, ln_g, ln_b, tm, seq):
    n = x2.shape[0]
    d = D_MODEL
    full = lambda shape: pl.BlockSpec(shape, lambda i: (0,) * len(shape))
    return pl.pallas_call(
        functools.partial(_mix_kernel, tm=tm),
        grid=(n // tm,),
        in_specs=[
            pl.BlockSpec((tm, d), lambda i: (i, 3)),
            pl.BlockSpec((tm, d), lambda i: (i, 4)),
            pl.BlockSpec((tm, d), lambda i: (i, 5)),
            pl.BlockSpec((tm, 3 * d), lambda i: (i, 2)),
            pl.BlockSpec((tm, d), lambda i: (i, 0)),
            pl.BlockSpec((tm, d), lambda i: (i, 0)),
            pl.BlockSpec((1, mkv3.shape[1], mkv3.shape[2]), lambda i: ((i * tm) // seq, 0, 0)),
            full(w_s.shape), full(b_s_t.shape),
            full((d, d)), full((d, d)), full((d, d)), full((d, d)),
            full((1, d)), full((1, d)),
        ],
        out_specs=pl.BlockSpec((tm, d), lambda i: (i, 0)),
        out_shape=jax.ShapeDtypeStruct((n, d), F32),
        scratch_shapes=[pltpu.VMEM((tm, d), BF16), pltpu.VMEM((tm, d), BF16)],
        compiler_params=pltpu.CompilerParams(
            dimension_semantics=("arbitrary",), vmem_limit_bytes=VMEM_LIMIT),
        name="mix",
    )(z2, z2, z2, z2, oda2, x2, mkv3, w_s, b_s_t, wa, wsg, wm, wo, ln_g, ln_b)


def _ffn_kernel(x_ref, wi_ref, wo_ref, g_ref, b_ref, o_ref, *, d_ff, tm):
    halves = [slice(h * MIX_HALF, (h + 1) * MIX_HALF) for h in range(tm // MIX_HALF)]
    hid = [_dot(x_ref[hs, :].astype(BF16), wi_ref[...]) for hs in halves]
    act = [((h[:, :d_ff] * _sigmoid(h[:, :d_ff])) * h[:, d_ff:]).astype(BF16) for h in hid]
    y = [_dot(a, wo_ref[...]) for a in act]
    for hi, hs in enumerate(halves):
        o_ref[hs, :] = _layer_norm(ALPHA * x_ref[hs, :] + y[hi], g_ref[...], b_ref[...])


def _ffn(x1, wi, wo, ln_g, ln_b, tm):
    n, d = x1.shape
    d_ff = wo.shape[0]
    full = lambda shape: pl.BlockSpec(shape, lambda i: (0,) * len(shape),
                                      pipeline_mode=pl.Buffered(1))
    return pl.pallas_call(
        functools.partial(_ffn_kernel, d_ff=d_ff, tm=tm),
        grid=(n // tm,),
        in_specs=[pl.BlockSpec((tm, d), lambda i: (i, 0)),
                  full(wi.shape), full(wo.shape), full((1, d)), full((1, d))],
        out_specs=pl.BlockSpec((tm, d), lambda i: (i, 0)),
        out_shape=jax.ShapeDtypeStruct((n, d), F32),
        compiler_params=pltpu.CompilerParams(
            dimension_semantics=("arbitrary",), vmem_limit_bytes=VMEM_LIMIT),
        name="ffn",
    )(x1, wi, wo, ln_g, ln_b)


def kernel(x, mem, positions, w_in, lambda_q1, lambda_k1, lambda_q2, lambda_k2, da_subln_g,
           sg_norm_g, sg_norm_b, sg_w_s, sg_b_s, w_mem_kv, w_br_attn, w_br_sg, w_br_mem, w_out,
           ln1_g, ln1_b, w_ffn_in, w_ffn_out, ln2_g, ln2_b):
    bsz, seq, d = x.shape
    n = bsz * seq
    depth = w_in.shape[0]
    assert d == D_MODEL and depth == DEPTH

    inv_freq = ROPE_THETA ** (-jnp.arange(ROT_HALF, dtype=F32) * 2.0 / ROT_DIM)
    lane_d = jnp.arange(LANES) % DA_HEAD_DIM
    invf = jnp.where(lane_d < ROT_DIM, inv_freq[lane_d % ROT_HALF], 0.0).reshape(1, LANES)
    pos2 = positions.reshape(n, 1)

    x2 = x.reshape(n, d)
    for l in range(depth):
        lambda_init = 0.8 - 0.6 * math.exp(-0.3 * l)
        row = lambda a: a[l].reshape(1, -1).astype(F32)
        z2 = _proj_in(x2, w_in[l].astype(BF16), pos2, invf, row(sg_norm_g), row(sg_norm_b), tm=1024)
        oda = _diff_attn(z2.reshape(bsz, seq, -1), row(lambda_q1), row(lambda_k1),
                         row(lambda_q2), row(lambda_k2), row(da_subln_g), tq=512,
                         lambda_init=lambda_init)
        mkv = _mem_kv(mem.reshape(-1, d), w_mem_kv[l].astype(BF16)).reshape(bsz, mem.shape[1], -1)
        x1 = _mix(z2, oda.reshape(n, -1), x2, mkv, sg_w_s[l], sg_b_s[l].T,
                  w_br_attn[l].astype(BF16), w_br_sg[l].astype(BF16), w_br_mem[l].astype(BF16),
                  w_out[l].astype(BF16), row(ln1_g), row(ln1_b), tm=512, seq=seq)
        x2 = _ffn(x1, w_ffn_in[l].astype(BF16), w_ffn_out[l].astype(BF16), row(ln2_g), row(ln2_b),
                  tm=512)
    return x2.reshape(bsz, seq, d)
```

```python
import functools
import math

import jax
import jax.numpy as jnp
from jax import lax
from jax.experimental import pallas as pl
from jax.experimental.pallas import tpu as pltpu

D_MODEL = 1024
DA_HEADS = 8
DA_HEAD_DIM = 64
DA_V_DIM = 2 * DA_HEAD_DIM
ROPE_THETA = 500000.0
ROT_DIM = DA_HEAD_DIM // 4
ROT_HALF = ROT_DIM // 2
SG_GROUPS = 8
SG_CHUNK = 128
SG_GROUP_DIM = 128
XA_HEADS = 4
XA_HEAD_DIM = 256
N_SECTIONS = 9
SECTIONS_PER_GROUP = 3
DEPTH = 1
ALPHA = (2 * DEPTH) ** 0.25
LN_EPS = 1e-5
RMS_EPS = 1e-5
LANES = 128
COL_CHUNK = 256
MIX_HALF = 256

VMEM_LIMIT = 56 * 1024 * 1024

F32 = jnp.float32
BF16 = jnp.bfloat16


def _dot(a, b):
    return jnp.dot(a, b, preferred_element_type=F32)


def _dot_nt(a, b):
    return lax.dot_general(a, b, (((1,), (1,)), ((), ())), preferred_element_type=F32)


def _sigmoid(x):
    return 0.5 + 0.5 * jnp.tanh(0.5 * x)


def _layer_norm(x, g, b):
    mu = jnp.mean(x, axis=-1, keepdims=True)
    xc = x - mu
    var = jnp.mean(xc * xc, axis=-1, keepdims=True)
    return xc * lax.rsqrt(var + LN_EPS) * g + b


def _proj_in_kernel(x_ref, w_ref, pos_ref, invf_ref, sgg_ref, sgb_ref, o_ref,
                    xb_ref, cos_ref, sin_ref, gel_ref):
    grp = pl.program_id(1)

    def run(sec, epilogue):
        for c in range(D_MODEL // COL_CHUNK):
            cs = slice(sec * D_MODEL + c * COL_CHUNK, sec * D_MODEL + (c + 1) * COL_CHUNK)
            epilogue(_dot(xb_ref[...], w_ref[:, cs]), cs)

    def plain(acc, cs):
        o_ref[:, cs] = acc.astype(BF16)

    def rope(scale):
        d = lax.broadcasted_iota(jnp.int32, (1, LANES), 1) % DA_HEAD_DIM
        sgn_lo = jnp.where(d < ROT_HALF, -scale, 0.0)
        sgn_hi = jnp.where((d >= ROT_HALF) & (d < ROT_DIM), scale, 0.0)

        def epilogue(acc, cs):
            cos = cos_ref[...] * scale
            sin_lo = sin_ref[...] * sgn_lo
            sin_hi = sin_ref[...] * sgn_hi
            for h in range(COL_CHUNK // LANES):
                t = acc[:, h * LANES:(h + 1) * LANES]
                t_up = pltpu.roll(t, LANES - ROT_HALF, 1)
                t_dn = pltpu.roll(t, ROT_HALF, 1)
                r = t * cos + t_up * sin_lo + t_dn * sin_hi
                o_ref[:, cs.start + h * LANES:cs.start + (h + 1) * LANES] = r.astype(BF16)
        return epilogue

    @pl.when(grp == 0)
    def _():
        xb_ref[...] = x_ref[...].astype(BF16)
        run(2, plain)
        ang = pos_ref[...].astype(F32) * invf_ref[...]
        cos_ref[...] = jnp.cos(ang)
        sin_ref[...] = jnp.sin(ang)
        run(0, rope(DA_HEAD_DIM ** -0.5 * math.log2(math.e)))
        run(1, rope(1.0))

    @pl.when(grp == 1)
    def _():
        def gelu_keep(acc, cs):
            gel_ref[:, cs.start - D_MODEL:cs.stop - D_MODEL] = jax.nn.gelu(acc)

        def gelu_out(acc, cs):
            o_ref[:, cs] = jax.nn.gelu(acc).astype(BF16)

        run(1, gelu_keep)
        run(0, gelu_out)
        o_ref[:, D_MODEL:2 * D_MODEL] = _layer_norm(
            gel_ref[...], sgg_ref[...], sgb_ref[...]).astype(BF16)
        run(2, plain)

    @pl.when(grp == 2)
    def _():
        def gate(acc, cs):
            o_ref[:, cs] = _sigmoid(acc).astype(BF16)
        for sec in range(SECTIONS_PER_GROUP):
            run(sec, gate)


def _proj_in(x2, w_in_b, pos2, invf, sgg, sgb, tm):
    n = x2.shape[0]
    gw = SECTIONS_PER_GROUP * D_MODEL
    return pl.pallas_call(
        _proj_in_kernel,
        grid=(n // tm, N_SECTIONS // SECTIONS_PER_GROUP),
        in_specs=[
            pl.BlockSpec((tm, D_MODEL), lambda i, j: (i, 0)),
            pl.BlockSpec((D_MODEL, gw), lambda i, j: (0, j)),
            pl.BlockSpec((tm, 1), lambda i, j: (i, 0)),
            pl.BlockSpec((1, LANES), lambda i, j: (0, 0)),
            pl.BlockSpec((1, D_MODEL), lambda i, j: (0, 0)),
            pl.BlockSpec((1, D_MODEL), lambda i, j: (0, 0)),
        ],
        out_specs=pl.BlockSpec((tm, gw), lambda i, j: (i, j)),
        out_shape=jax.ShapeDtypeStruct((n, N_SECTIONS * D_MODEL), BF16),
        scratch_shapes=[
            pltpu.VMEM((tm, D_MODEL), BF16),
            pltpu.VMEM((tm, LANES), F32),
            pltpu.VMEM((tm, LANES), F32),
            pltpu.VMEM((tm, D_MODEL), F32),
        ],
        compiler_params=pltpu.CompilerParams(
            dimension_semantics=("arbitrary", "arbitrary"), vmem_limit_bytes=VMEM_LIMIT),
        name="proj_in",
    )(x2, w_in_b, pos2, invf, sgg, sgb)


def _diff_attn_kernel(q_ref, k_ref, v_ref, lq1_ref, lk1_ref, lq2_ref, lk2_ref, g_ref, o_ref,
                      qs_ref, s0_ref, s1_ref, mx0_ref, mx1_ref, m_ref, l_ref, acc_ref,
                      *, tq, tk, nq, lambda_init):
    s_ref = (s0_ref, s1_ref)
    mx_ref = (mx0_ref, mx1_ref)
    rows = 2 * tq
    lane = lax.broadcasted_iota(jnp.int32, (1, LANES), 1)

    def scores(jb):
        start = pl.multiple_of(jb * tk, tk)
        kb = k_ref[0, pl.ds(start, tk), :]
        s = _dot_nt(qs_ref[...], kb)
        smax = s[:, 0:LANES]
        for g in range(1, tk // LANES):
            smax = jnp.maximum(smax, s[:, g * LANES:(g + 1) * LANES])
        return s, smax

    def begin(qi):
        q = q_ref[0, pl.ds(pl.multiple_of(qi * tq, tq), tq), :]
        qs_ref[0:tq, :] = jnp.where(lane < DA_HEAD_DIM, q, jnp.zeros_like(q))
        qs_ref[tq:rows, :] = jnp.where(lane >= DA_HEAD_DIM, q, jnp.zeros_like(q))
        s_ref[0][...], mx_ref[0][...] = scores(0)
        m_ref[...] = jnp.full(m_ref.shape, -jnp.inf, F32)
        l_ref[...] = jnp.zeros(l_ref.shape, F32)
        acc_ref[...] = jnp.zeros(acc_ref.shape, F32)

    def finish(qi):
        lam = (jnp.exp(jnp.sum(lq1_ref[...] * lk1_ref[...], axis=-1, keepdims=True))
               - jnp.exp(jnp.sum(lq2_ref[...] * lk2_ref[...], axis=-1, keepdims=True))
               + lambda_init)
        o = (acc_ref[0:tq, :] / l_ref[0:tq, :]
             - lam * (acc_ref[tq:rows, :] / l_ref[tq:rows, :]))
        ms = jnp.mean(o * o, axis=-1, keepdims=True)
        o = o * lax.rsqrt(ms + RMS_EPS) * g_ref[...] * (1.0 - lambda_init)
        o_ref[0, pl.ds(pl.multiple_of(qi * tq, tq), tq), :] = o.astype(BF16)

    def softmax_pv(jb, slot, width, mask_from, smax=None):
        ngrp = width // LANES
        start = pl.multiple_of(jb * tk, tk)
        vb = v_ref[0, pl.ds(start, width), :]
        row = lax.broadcasted_iota(jnp.int32, (rows, LANES), 0) & (tq - 1)
        col = lax.broadcasted_iota(jnp.int32, (rows, LANES), 1)

        def sgrp(g):
            x = s_ref[slot][:, g * LANES:(g + 1) * LANES]
            if g >= mask_from:
                x = jnp.where(col + (g - mask_from) * LANES <= row, x, -jnp.inf)
            return x

        if smax is None:
            smax = sgrp(0)
            for g in range(1, ngrp):
                smax = jnp.maximum(smax, sgrp(g))
        m_old = m_ref[...]
        m_new = jnp.maximum(m_old, jnp.max(smax, axis=-1, keepdims=True))
        m_ref[...] = m_new
        a = jnp.exp2(m_old - m_new)
        p = jnp.concatenate([jnp.exp2((sgrp(g) - m_new).astype(BF16)) for g in range(ngrp)],
                            axis=1)
        v1 = jnp.concatenate([vb, jnp.ones((width, LANES), BF16)], axis=1)
        pv = _dot(p, v1)
        acc_ref[...] = a * acc_ref[...] + pv[:, :DA_V_DIM]
        l_ref[...] = a * l_ref[...] + pv[:, DA_V_DIM:]

    def step(jb, slot):
        s_ref[1 - slot][...], mx_ref[1 - slot][...] = scores(jb + 1)
        softmax_pv(jb, slot, tk, tk // LANES, smax=mx_ref[slot][...])

    def pair(p, carry):
        step(2 * p, 0)
        step(2 * p + 1, 1)
        return carry

    def query_block(i, carry):
        nfull = lax.shift_right_logical(i, 1)
        lax.fori_loop(0, lax.shift_right_logical(nfull, 1), pair, 0)

        @pl.when((nfull & 1) == 1)
        def _():
            step(nfull - 1, 0)

        for slot in range(2):
            @pl.when(((i & 1) == 0) & ((nfull & 1) == slot))
            def _():
                softmax_pv(nfull, slot, tq, 0)

            @pl.when(((i & 1) == 1) & ((nfull & 1) == slot))
            def _():
                softmax_pv(nfull, slot, tk, tq // LANES)

        @pl.when(i + 1 < nq)
        def _():
            finish(i)
            begin(i + 1)

        @pl.when(i + 1 == nq)
        def _():
            finish(i)

        return carry

    begin(0)
    lax.fori_loop(0, nq, query_block, 0)


def _diff_attn(z3, lq1, lk1, lq2, lk2, subln_g, tq, lambda_init):
    bsz, seq, _ = z3.shape
    tk = 2 * tq
    kern = functools.partial(_diff_attn_kernel, tq=tq, tk=tk, nq=seq // tq,
                             lambda_init=lambda_init)
    vec = lambda w: pl.BlockSpec((1, w), lambda b, h: (0, 0))
    return pl.pallas_call(
        kern,
        grid=(bsz, DA_HEADS),
        in_specs=[
            pl.BlockSpec((1, seq, LANES), lambda b, h: (b, 0, h)),
            pl.BlockSpec((1, seq, LANES), lambda b, h: (b, 0, DA_HEADS + h)),
            pl.BlockSpec((1, seq, LANES), lambda b, h: (b, 0, 2 * DA_HEADS + h)),
            vec(DA_HEAD_DIM), vec(DA_HEAD_DIM), vec(DA_HEAD_DIM), vec(DA_HEAD_DIM),
            vec(DA_V_DIM),
        ],
        out_specs=pl.BlockSpec((1, seq, LANES), lambda b, h: (b, 0, h)),
        out_shape=jax.ShapeDtypeStruct((bsz, seq, DA_HEADS * DA_V_DIM), BF16),
        scratch_shapes=[
            pltpu.VMEM((2 * tq, LANES), BF16),
            pltpu.VMEM((2 * tq, tk), F32),
            pltpu.VMEM((2 * tq, tk), F32),
            pltpu.VMEM((2 * tq, LANES), F32),
            pltpu.VMEM((2 * tq, LANES), F32),
            pltpu.VMEM((2 * tq, LANES), F32),
            pltpu.VMEM((2 * tq, LANES), F32),
            pltpu.VMEM((2 * tq, DA_V_DIM), F32),
        ],
        compiler_params=pltpu.CompilerParams(
            dimension_semantics=("arbitrary", "arbitrary"),
            vmem_limit_bytes=VMEM_LIMIT),
        name="diff_attn",
    )(z3, z3, z3, lq1, lk1, lq2, lk2, subln_g)


def _mem_kv_kernel(m_ref, w_ref, o_ref):
    o_ref[...] = _dot(m_ref[...].astype(BF16), w_ref[...]).astype(BF16)


def _mem_kv(mem2, w_b):
    rows, width = mem2.shape[0], w_b.shape[1]
    tn = 1024
    return pl.pallas_call(
        _mem_kv_kernel,
        grid=(width // tn,),
        in_specs=[pl.BlockSpec((rows, D_MODEL), lambda j: (0, 0)),
                  pl.BlockSpec((D_MODEL, tn), lambda j: (0, j))],
        out_specs=pl.BlockSpec((rows, tn), lambda j: (0, j)),
        out_shape=jax.ShapeDtypeStruct((rows, width), BF16),
        compiler_params=pltpu.CompilerParams(
            dimension_semantics=("arbitrary",), vmem_limit_bytes=VMEM_LIMIT),
        name="mem_kv",
    )(mem2, w_b)


def _mix_kernel(u_ref, v_ref, xq_ref, gate_ref, oda_ref, x_ref, mkv_ref, ws_ref, bs_ref,
                wa_ref, wsg_ref, wm_ref, wo_ref, g1_ref, b1_ref, o_ref, osg_ref, oxa_ref, *, tm):
    halves = [slice(h * MIX_HALF, (h + 1) * MIX_HALF) for h in range(tm // MIX_HALF)]
    d = D_MODEL
    row = lax.broadcasted_iota(jnp.int32, (SG_CHUNK, SG_CHUNK), 0)
    col = lax.broadcasted_iota(jnp.int32, (SG_CHUNK, SG_CHUNK), 1)
    ws = [jnp.where(col <= row, ws_ref[g], 0.0).astype(BF16) for g in range(SG_GROUPS)]

    def chunks(hs):
        return [slice(r, r + SG_CHUNK) for r in range(hs.start, hs.stop, SG_CHUNK)]

    def xa_cols(hd):
        return slice(hd * XA_HEAD_DIM, (hd + 1) * XA_HEAD_DIM)

    sg = {}
    for hs in halves:
        for g in range(SG_GROUPS):
            cs = slice(g * SG_GROUP_DIM, (g + 1) * SG_GROUP_DIM)
            for rs in chunks(hs):
                sg[rs.start, g] = _dot(ws[g], v_ref[rs, cs]) + bs_ref[:, g:g + 1]
    t_attn = [_dot(oda_ref[hs, :], wa_ref[...]) for hs in halves]
    xs = [[_dot_nt(xq_ref[hs, xa_cols(hd)], mkv_ref[0, :, xa_cols(hd)]) * (XA_HEAD_DIM ** -0.5)
           for hd in range(XA_HEADS)] for hs in halves]
    for hs in halves:
        for g in range(SG_GROUPS):
            cs = slice(g * SG_GROUP_DIM, (g + 1) * SG_GROUP_DIM)
            for rs in chunks(hs):
                osg_ref[rs, cs] = (u_ref[rs, cs].astype(F32) * sg[rs.start, g]).astype(BF16)
    t_sg = [_dot(osg_ref[hs, :], wsg_ref[...]) for hs in halves]
    for hi, hs in enumerate(halves):
        for hd in range(XA_HEADS):
            s = xs[hi][hd]
            p = jnp.exp(s - jnp.max(s, axis=-1, keepdims=True))
            p = p / jnp.sum(p, axis=-1, keepdims=True)
            vs = slice(XA_HEADS * XA_HEAD_DIM + hd * XA_HEAD_DIM,
                       XA_HEADS * XA_HEAD_DIM + (hd + 1) * XA_HEAD_DIM)
            oxa_ref[hs, xa_cols(hd)] = _dot(p.astype(BF16), mkv_ref[0, :, vs]).astype(BF16)
    t_mem = [_dot(oxa_ref[hs, :], wm_ref[...]) for hs in halves]
    merged = [(gate_ref[hs, 0:d].astype(F32) * t_attn[hi]
               + gate_ref[hs, d:2 * d].astype(F32) * t_sg[hi]
               + gate_ref[hs, 2 * d:3 * d].astype(F32) * t_mem[hi]).astype(BF16)
              for hi, hs in enumerate(halves)]
    y = [_dot(m, wo_ref[...]) for m in merged]
    for hi, hs in enumerate(halves):
        o_ref[hs, :] = _layer_norm(ALPHA * x_ref[hs, :] + y[hi], g1_ref[...], b1_ref[...])


def _mix(z2, oda2, x2, mkv3, w_s, b_s_t, wa, wsg, wm, wo, ln_g, ln_b, tm, seq):
    n = x2.shape[0]
    d = D_MODEL
    full = lambda shape: pl.BlockSpec(shape, lambda i: (0,) * len(shape))
    return pl.pallas_call(
        functools.partial(_mix_kernel, tm=tm),
        grid=(n // tm,),
        in_specs=[
            pl.BlockSpec((tm, d), lambda i: (i, 3)),
            pl.BlockSpec((tm, d), lambda i: (i, 4)),
            pl.BlockSpec((tm, d), lambda i: (i, 5)),
            pl.BlockSpec((tm, 3 * d), lambda i: (i, 2)),
            pl.BlockSpec((tm, d), lambda i: (i, 0)),
            pl.BlockSpec((tm, d), lambda i: (i, 0)),
            pl.BlockSpec((1, mkv3.shape[1], mkv3.shape[2]), lambda i: ((i * tm) // seq, 0, 0)),
            full(w_s.shape), full(b_s_t.shape),
            full((d, d)), full((d, d)), full((d, d)), full((d, d)),
            full((1, d)), full((1, d)),
        ],
        out_specs=pl.BlockSpec((tm, d), lambda i: (i, 0)),
        out_shape=jax.ShapeDtypeStruct((n, d), F32),
        scratch_shapes=[pltpu.VMEM((tm, d), BF16), pltpu.VMEM((tm, d), BF16)],
        compiler_params=pltpu.CompilerParams(
            dimension_semantics=("arbitrary",), vmem_limit_bytes=VMEM_LIMIT),
        name="mix",
    )(z2, z2, z2, z2, oda2, x2, mkv3, w_s, b_s_t, wa, wsg, wm, wo, ln_g, ln_b)


def _ffn_kernel(x_ref, wi_ref, wo_ref, g_ref, b_ref, o_ref, *, d_ff, tm):
    halves = [slice(h * MIX_HALF, (h + 1) * MIX_HALF) for h in range(tm // MIX_HALF)]
    hid = [_dot(x_ref[hs, :].astype(BF16), wi_ref[...]) for hs in halves]
    act = [((h[:, :d_ff] * _sigmoid(h[:, :d_ff])) * h[:, d_ff:]).astype(BF16) for h in hid]
    y = [_dot(a, wo_ref[...]) for a in act]
    for hi, hs in enumerate(halves):
        o_ref[hs, :] = _layer_norm(ALPHA * x_ref[hs, :] + y[hi], g_ref[...], b_ref[...])


def _ffn(x1, wi, wo, ln_g, ln_b, tm):
    n, d = x1.shape
    d_ff = wo.shape[0]
    full = lambda shape: pl.BlockSpec(shape, lambda i: (0,) * len(shape),
                                      pipeline_mode=pl.Buffered(1))
    return pl.pallas_call(
        functools.partial(_ffn_kernel, d_ff=d_ff, tm=tm),
        grid=(n // tm,),
        in_specs=[pl.BlockSpec((tm, d), lambda i: (i, 0)),
                  full(wi.shape), full(wo.shape), full((1, d)), full((1, d))],
        out_specs=pl.BlockSpec((tm, d), lambda i: (i, 0)),
        out_shape=jax.ShapeDtypeStruct((n, d), F32),
        compiler_params=pltpu.CompilerParams(
            dimension_semantics=("arbitrary",), vmem_limit_bytes=VMEM_LIMIT),
        name="ffn",
    )(x1, wi, wo, ln_g, ln_b)


def kernel(x, mem, positions, w_in, lambda_q1, lambda_k1, lambda_q2, lambda_k2, da_subln_g,
           sg_norm_g, sg_norm_b, sg_w_s, sg_b_s, w_mem_kv, w_br_attn, w_br_sg, w_br_mem, w_out,
           ln1_g, ln1_b, w_ffn_in, w_ffn_out, ln2_g, ln2_b):
    bsz, seq, d = x.shape
    n = bsz * seq
    depth = w_in.shape[0]
    assert d == D_MODEL and depth == DEPTH

    inv_freq = ROPE_THETA ** (-jnp.arange(ROT_HALF, dtype=F32) * 2.0 / ROT_DIM)
    lane_d = jnp.arange(LANES) % DA_HEAD_DIM
    invf = jnp.where(lane_d < ROT_DIM, inv_freq[lane_d % ROT_HALF], 0.0).reshape(1, LANES)
    pos2 = positions.reshape(n, 1)

    x2 = x.reshape(n, d)
    for l in range(depth):
        lambda_init = 0.8 - 0.6 * math.exp(-0.3 * l)
        row = lambda a: a[l].reshape(1, -1).astype(F32)
        z2 = _proj_in(x2, w_in[l].astype(BF16), pos2, invf, row(sg_norm_g), row(sg_norm_b), tm=1024)
        oda = _diff_attn(z2.reshape(bsz, seq, -1), row(lambda_q1), row(lambda_k1),
                         row(lambda_q2), row(lambda_k2), row(da_subln_g), tq=512,
                         lambda_init=lambda_init)
        mkv = _mem_kv(mem.reshape(-1, d), w_mem_kv[l].astype(BF16)).reshape(bsz, mem.shape[1], -1)
        x1 = _mix(z2, oda.reshape(n, -1), x2, mkv, sg_w_s[l], sg_b_s[l].T,
                  w_br_attn[l].astype(BF16), w_br_sg[l].astype(BF16), w_br_mem[l].astype(BF16),
                  w_out[l].astype(BF16), row(ln1_g), row(ln1_b), tm=512, seq=seq)
        x2 = _ffn(x1, w_ffn_in[l].astype(BF16), w_ffn_out[l].astype(BF16), row(ln2_g), row(ln2_b),
                  tm=512)
    return x2.reshape(bsz, seq, d)
```

```python
import functools
import math

import jax
import jax.numpy as jnp
from jax import lax
from jax.experimental import pallas as pl
from jax.experimental.pallas import tpu as pltpu

D_MODEL = 1024
DA_HEADS = 8
DA_HEAD_DIM = 64
DA_V_DIM = 2 * DA_HEAD_DIM
ROPE_THETA = 500000.0
ROT_DIM = DA_HEAD_DIM // 4
ROT_HALF = ROT_DIM // 2
SG_GROUPS = 8
SG_CHUNK = 128
SG_GROUP_DIM = 128
XA_HEADS = 4
XA_HEAD_DIM = 256
N_SECTIONS = 9
SECTIONS_PER_GROUP = 3
DEPTH = 1
ALPHA = (2 * DEPTH) ** 0.25
LN_EPS = 1e-5
RMS_EPS = 1e-5
LANES = 128
COL_CHUNK = 256
MIX_HALF = 256

V7X_VMEM_BYTES = 64 * 1024 * 1024
VMEM_LIMIT = V7X_VMEM_BYTES * 7 // 8

PROJ_ROWS = 1024
ATTN_Q_ROWS = 512
ATTN_K_ROWS = 1024
MIX_ROWS = 512
FFN_ROWS = 512

F32 = jnp.float32
BF16 = jnp.bfloat16


def _dot(a, b):
    return jnp.dot(a, b, preferred_element_type=F32)


def _dot_nt(a, b):
    return lax.dot_general(a, b, (((1,), (1,)), ((), ())), preferred_element_type=F32)


def _sigmoid(x):
    return 0.5 + 0.5 * jnp.tanh(0.5 * x)


def _layer_norm(x, g, b):
    mu = jnp.mean(x, axis=-1, keepdims=True)
    xc = x - mu
    var = jnp.mean(xc * xc, axis=-1, keepdims=True)
    return xc * lax.rsqrt(var + LN_EPS) * g + b


def _proj_in_kernel(x_ref, w_ref, pos_ref, invf_ref, sgg_ref, sgb_ref, o_ref,
                    xb_ref, cos_ref, sin_ref, gel_ref):
    grp = pl.program_id(1)

    def run(sec, epilogue):
        for c in range(D_MODEL // COL_CHUNK):
            cs = slice(sec * D_MODEL + c * COL_CHUNK, sec * D_MODEL + (c + 1) * COL_CHUNK)
            epilogue(_dot(xb_ref[...], w_ref[:, cs]), cs)

    def plain(acc, cs):
        o_ref[:, cs] = acc.astype(BF16)

    def rope(scale):
        d = lax.broadcasted_iota(jnp.int32, (1, LANES), 1) % DA_HEAD_DIM
        sgn_lo = jnp.where(d < ROT_HALF, -scale, 0.0)
        sgn_hi = jnp.where((d >= ROT_HALF) & (d < ROT_DIM), scale, 0.0)

        def epilogue(acc, cs):
            cos = cos_ref[...] * scale
            sin_lo = sin_ref[...] * sgn_lo
            sin_hi = sin_ref[...] * sgn_hi
            for h in range(COL_CHUNK // LANES):
                t = acc[:, h * LANES:(h + 1) * LANES]
                t_up = pltpu.roll(t, LANES - ROT_HALF, 1)
                t_dn = pltpu.roll(t, ROT_HALF, 1)
                r = t * cos + t_up * sin_lo + t_dn * sin_hi
                o_ref[:, cs.start + h * LANES:cs.start + (h + 1) * LANES] = r.astype(BF16)
        return epilogue

    @pl.when(grp == 0)
    def _():
        xb_ref[...] = x_ref[...].astype(BF16)
        run(2, plain)
        ang = pos_ref[...].astype(F32) * invf_ref[...]
        cos_ref[...] = jnp.cos(ang)
        sin_ref[...] = jnp.sin(ang)
        run(0, rope(DA_HEAD_DIM ** -0.5 * math.log2(math.e)))
        run(1, rope(1.0))

    @pl.when(grp == 1)
    def _():
        def gelu_keep(acc, cs):
            gel_ref[:, cs.start - D_MODEL:cs.stop - D_MODEL] = jax.nn.gelu(acc)

        def gelu_out(acc, cs):
            o_ref[:, cs] = jax.nn.gelu(acc).astype(BF16)

        run(1, gelu_keep)
        run(0, gelu_out)
        o_ref[:, D_MODEL:2 * D_MODEL] = _layer_norm(
            gel_ref[...], sgg_ref[...], sgb_ref[...]).astype(BF16)
        run(2, plain)

    @pl.when(grp == 2)
    def _():
        def gate(acc, cs):
            o_ref[:, cs] = _sigmoid(acc).astype(BF16)
        for sec in range(SECTIONS_PER_GROUP):
            run(sec, gate)


def _proj_in(x2, w_in_b, pos2, invf, sgg, sgb, tm):
    n = x2.shape[0]
    gw = SECTIONS_PER_GROUP * D_MODEL
    return pl.pallas_call(
        _proj_in_kernel,
        grid=(n // tm, N_SECTIONS // SECTIONS_PER_GROUP),
        in_specs=[
            pl.BlockSpec((tm, D_MODEL), lambda i, j: (i, 0)),
            pl.BlockSpec((D_MODEL, gw), lambda i, j: (0, j)),
            pl.BlockSpec((tm, 1), lambda i, j: (i, 0)),
            pl.BlockSpec((1, LANES), lambda i, j: (0, 0)),
            pl.BlockSpec((1, D_MODEL), lambda i, j: (0, 0)),
            pl.BlockSpec((1, D_MODEL), lambda i, j: (0, 0)),
        ],
        out_specs=pl.BlockSpec((tm, gw), lambda i, j: (i, j)),
        out_shape=jax.ShapeDtypeStruct((n, N_SECTIONS * D_MODEL), BF16),
        scratch_shapes=[
            pltpu.VMEM((tm, D_MODEL), BF16),
            pltpu.VMEM((tm, LANES), F32),
            pltpu.VMEM((tm, LANES), F32),
            pltpu.VMEM((tm, D_MODEL), F32),
        ],
        compiler_params=pltpu.CompilerParams(
            dimension_semantics=("arbitrary", "arbitrary"), vmem_limit_bytes=VMEM_LIMIT),
        name="proj_in",
    )(x2, w_in_b, pos2, invf, sgg, sgb)


def _diff_attn_kernel(q_ref, k_ref, v_ref, lq1_ref, lk1_ref, lq2_ref, lk2_ref, g_ref, o_ref,
                      qs_ref, s0_ref, s1_ref, mx0_ref, mx1_ref, m_ref, l_ref, acc_ref,
                      *, tq, tk, nq, lambda_init):
    s_ref = (s0_ref, s1_ref)
    mx_ref = (mx0_ref, mx1_ref)
    rows = 2 * tq
    lane = lax.broadcasted_iota(jnp.int32, (1, LANES), 1)

    def scores(jb):
        start = pl.multiple_of(jb * tk, tk)
        kb = k_ref[0, pl.ds(start, tk), :]
        s = _dot_nt(qs_ref[...], kb)
        smax = s[:, 0:LANES]
        for g in range(1, tk // LANES):
            smax = jnp.maximum(smax, s[:, g * LANES:(g + 1) * LANES])
        return s, smax

    def begin(qi):
        q = q_ref[0, pl.ds(pl.multiple_of(qi * tq, tq), tq), :]
        qs_ref[0:tq, :] = jnp.where(lane < DA_HEAD_DIM, q, jnp.zeros_like(q))
        qs_ref[tq:rows, :] = jnp.where(lane >= DA_HEAD_DIM, q, jnp.zeros_like(q))
        s_ref[0][...], mx_ref[0][...] = scores(0)
        m_ref[...] = jnp.full(m_ref.shape, -jnp.inf, F32)
        l_ref[...] = jnp.zeros(l_ref.shape, F32)
        acc_ref[...] = jnp.zeros(acc_ref.shape, F32)

    def finish(qi):
        lam = (jnp.exp(jnp.sum(lq1_ref[...] * lk1_ref[...], axis=-1, keepdims=True))
               - jnp.exp(jnp.sum(lq2_ref[...] * lk2_ref[...], axis=-1, keepdims=True))
               + lambda_init)
        o = (acc_ref[0:tq, :] / l_ref[0:tq, :]
             - lam * (acc_ref[tq:rows, :] / l_ref[tq:rows, :]))
        ms = jnp.mean(o * o, axis=-1, keepdims=True)
        o = o * lax.rsqrt(ms + RMS_EPS) * g_ref[...] * (1.0 - lambda_init)
        o_ref[0, pl.ds(pl.multiple_of(qi * tq, tq), tq), :] = o.astype(BF16)

    def softmax_pv(jb, slot, width, mask_from, smax=None):
        ngrp = width // LANES
        start = pl.multiple_of(jb * tk, tk)
        vb = v_ref[0, pl.ds(start, width), :]
        row = lax.broadcasted_iota(jnp.int32, (rows, LANES), 0) & (tq - 1)
        col = lax.broadcasted_iota(jnp.int32, (rows, LANES), 1)

        def sgrp(g):
            x = s_ref[slot][:, g * LANES:(g + 1) * LANES]
            if g >= mask_from:
                x = jnp.where(col + (g - mask_from) * LANES <= row, x, -jnp.inf)
            return x

        if smax is None:
            smax = sgrp(0)
            for g in range(1, ngrp):
                smax = jnp.maximum(smax, sgrp(g))
        m_old = m_ref[...]
        m_new = jnp.maximum(m_old, jnp.max(smax, axis=-1, keepdims=True))
        m_ref[...] = m_new
        a = jnp.exp2(m_old - m_new)
        p = jnp.concatenate([jnp.exp2((sgrp(g) - m_new).astype(BF16)) for g in range(ngrp)],
                            axis=1)
        v1 = jnp.concatenate([vb, jnp.ones((width, LANES), BF16)], axis=1)
        pv = _dot(p, v1)
        acc_ref[...] = a * acc_ref[...] + pv[:, :DA_V_DIM]
        l_ref[...] = a * l_ref[...] + pv[:, DA_V_DIM:]

    def step(jb, slot):
        s_ref[1 - slot][...], mx_ref[1 - slot][...] = scores(jb + 1)
        softmax_pv(jb, slot, tk, tk // LANES, smax=mx_ref[slot][...])

    def pair(p, carry):
        step(2 * p, 0)
        step(2 * p + 1, 1)
        return carry

    def query_block(i, carry):
        if tk == tq:
            nfull = i
        else:
            nfull = lax.shift_right_logical(i, 1)
        lax.fori_loop(0, lax.shift_right_logical(nfull, 1), pair, 0)

        def last_blocks(slot, width, mask_from):
            if slot == 1:
                step(nfull - 1, 0)
            softmax_pv(nfull, slot, width, mask_from)

        for slot in range(2):
            if tk == tq:
                pl.when((nfull & 1) == slot)(functools.partial(last_blocks, slot, tq, 0))
            else:
                pl.when(((i & 1) == 0) & ((nfull & 1) == slot))(
                    functools.partial(last_blocks, slot, tq, 0))
                pl.when(((i & 1) == 1) & ((nfull & 1) == slot))(
                    functools.partial(last_blocks, slot, tk, tq // LANES))

        @pl.when(i + 1 < nq)
        def _():
            finish(i)
            begin(i + 1)

        @pl.when(i + 1 == nq)
        def _():
            finish(i)

        return carry

    begin(0)
    lax.fori_loop(0, nq, query_block, 0)


def _diff_attn(z3, lq1, lk1, lq2, lk2, subln_g, tq, tk, lambda_init):
    bsz, seq, _ = z3.shape
    assert tk in (tq, 2 * tq)
    kern = functools.partial(_diff_attn_kernel, tq=tq, tk=tk, nq=seq // tq,
                             lambda_init=lambda_init)
    vec = lambda w: pl.BlockSpec((1, w), lambda b, h: (0, 0))
    return pl.pallas_call(
        kern,
        grid=(bsz, DA_HEADS),
        in_specs=[
            pl.BlockSpec((1, seq, LANES), lambda b, h: (b, 0, h)),
            pl.BlockSpec((1, seq, LANES), lambda b, h: (b, 0, DA_HEADS + h)),
            pl.BlockSpec((1, seq, LANES), lambda b, h: (b, 0, 2 * DA_HEADS + h)),
            vec(DA_HEAD_DIM), vec(DA_HEAD_DIM), vec(DA_HEAD_DIM), vec(DA_HEAD_DIM),
            vec(DA_V_DIM),
        ],
        out_specs=pl.BlockSpec((1, seq, LANES), lambda b, h: (b, 0, h)),
        out_shape=jax.ShapeDtypeStruct((bsz, seq, DA_HEADS * DA_V_DIM), BF16),
        scratch_shapes=[
            pltpu.VMEM((2 * tq, LANES), BF16),
            pltpu.VMEM((2 * tq, tk), F32),
            pltpu.VMEM((2 * tq, tk), F32),
            pltpu.VMEM((2 * tq, LANES), F32),
            pltpu.VMEM((2 * tq, LANES), F32),
            pltpu.VMEM((2 * tq, LANES), F32),
            pltpu.VMEM((2 * tq, LANES), F32),
            pltpu.VMEM((2 * tq, DA_V_DIM), F32),
        ],
        compiler_params=pltpu.CompilerParams(
            dimension_semantics=("arbitrary", "arbitrary"),
            vmem_limit_bytes=VMEM_LIMIT),
        name="diff_attn",
    )(z3, z3, z3, lq1, lk1, lq2, lk2, subln_g)


def _mem_kv_kernel(m_ref, w_ref, o_ref):
    o_ref[...] = _dot(m_ref[...].astype(BF16), w_ref[...]).astype(BF16)


def _mem_kv(mem2, w_b):
    rows, width = mem2.shape[0], w_b.shape[1]
    tn = 1024
    return pl.pallas_call(
        _mem_kv_kernel,
        grid=(width // tn,),
        in_specs=[pl.BlockSpec((rows, D_MODEL), lambda j: (0, 0)),
                  pl.BlockSpec((D_MODEL, tn), lambda j: (0, j))],
        out_specs=pl.BlockSpec((rows, tn), lambda j: (0, j)),
        out_shape=jax.ShapeDtypeStruct((rows, width), BF16),
        compiler_params=pltpu.CompilerParams(
            dimension_semantics=("arbitrary",), vmem_limit_bytes=VMEM_LIMIT),
        name="mem_kv",
    )(mem2, w_b)


def _mix_kernel(u_ref, v_ref, xq_ref, gate_ref, oda_ref, x_ref, mkv_ref, ws_ref, bs_ref,
                wa_ref, wsg_ref, wm_ref, wo_ref, g1_ref, b1_ref, o_ref, osg_ref, oxa_ref, *, tm):
    halves = [slice(h * MIX_HALF, (h + 1) * MIX_HALF) for h in range(tm // MIX_HALF)]
    d = D_MODEL
    row = lax.broadcasted_iota(jnp.int32, (SG_CHUNK, SG_CHUNK), 0)
    col = lax.broadcasted_iota(jnp.int32, (SG_CHUNK, SG_CHUNK), 1)
    ws = [jnp.where(col <= row, ws_ref[g], 0.0).astype(BF16) for g in range(SG_GROUPS)]

    def chunks(hs):
        return [slice(r, r + SG_CHUNK) for r in range(hs.start, hs.stop, SG_CHUNK)]

    def xa_cols(hd):
        return slice(hd * XA_HEAD_DIM, (hd + 1) * XA_HEAD_DIM)

    sg = {}
    for hs in halves:
        for g in range(SG_GROUPS):
            cs = slice(g * SG_GROUP_DIM, (g + 1) * SG_GROUP_DIM)
            for rs in chunks(hs):
                sg[rs.start, g] = _dot(ws[g], v_ref[rs, cs]) + bs_ref[:, g:g + 1]
    t_attn = [_dot(oda_ref[hs, :], wa_ref[...]) for hs in halves]
    xs = [[_dot_nt(xq_ref[hs, xa_cols(hd)], mkv_ref[0, :, xa_cols(hd)]) * (XA_HEAD_DIM ** -0.5)
           for hd in range(XA_HEADS)] for hs in halves]
    for hs in halves:
        for g in range(SG_GROUPS):
            cs = slice(g * SG_GROUP_DIM, (g + 1) * SG_GROUP_DIM)
            for rs in chunks(hs):
                osg_ref[rs, cs] = (u_ref[rs, cs].astype(F32) * sg[rs.start, g]).astype(BF16)
    t_sg = [_dot(osg_ref[hs, :], wsg_ref[...]) for hs in halves]
    for hi, hs in enumerate(halves):
        for hd in range(XA_HEADS):
            s = xs[hi][hd]
            p = jnp.exp(s - jnp.max(s, axis=-1, keepdims=True))
            p = p / jnp.sum(p, axis=-1, keepdims=True)
            vs = slice(XA_HEADS * XA_HEAD_DIM + hd * XA_HEAD_DIM,
                       XA_HEADS * XA_HEAD_DIM + (hd + 1) * XA_HEAD_DIM)
            oxa_ref[hs, xa_cols(hd)] = _dot(p.astype(BF16), mkv_ref[0, :, vs]).astype(BF16)
    t_mem = [_dot(oxa_ref[hs, :], wm_ref[...]) for hs in halves]
    merged = [(gate_ref[hs, 0:d].astype(F32) * t_attn[hi]
               + gate_ref[hs, d:2 * d].astype(F32) * t_sg[hi]
               + gate_ref[hs, 2 * d:3 * d].astype(F32) * t_mem[hi]).astype(BF16)
              for hi, hs in enumerate(halves)]
    y = [_dot(m, wo_ref[...]) for m in merged]
    for hi, hs in enumerate(halves):
        o_ref[hs, :] = _layer_norm(ALPHA * x_ref[hs, :] + y[hi], g1_ref[...], b1_ref[...])


def _mix(z2, oda2, x2, mkv3, w_s, b_s_t, wa, wsg, wm, wo, ln_g, ln_b, tm, seq):
    n = x2.shape[0]
    d = D_MODEL
    full = lambda shape: pl.BlockSpec(shape, lambda i: (0,) * len(shape))
    return pl.pallas_call(
        functools.partial(_mix_kernel, tm=tm),
        grid=(n // tm,),
        in_specs=[
            pl.BlockSpec((tm, d), lambda i: (i, 3)),
            pl.BlockSpec((tm, d), lambda i: (i, 4)),
            pl.BlockSpec((tm, d), lambda i: (i, 5)),
            pl.BlockSpec((tm, 3 * d), lambda i: (i, 2)),
            pl.BlockSpec((tm, d), lambda i: (i, 0)),
            pl.BlockSpec((tm, d), lambda i: (i, 0)),
            pl.BlockSpec((1, mkv3.shape[1], mkv3.shape[2]), lambda i: ((i * tm) // seq, 0, 0)),
            full(w_s.shape), full(b_s_t.shape),
            full((d, d)), full((d, d)), full((d, d)), full((d, d)),
            full((1, d)), full((1, d)),
        ],
        out_specs=pl.BlockSpec((tm, d), lambda i: (i, 0)),
        out_shape=jax.ShapeDtypeStruct((n, d), F32),
        scratch_shapes=[pltpu.VMEM((tm, d), BF16), pltpu.VMEM((tm, d), BF16)],
        compiler_params=pltpu.CompilerParams(
            dimension_semantics=("arbitrary",), vmem_limit_bytes=VMEM_LIMIT),
        name="mix",
    )(z2, z2, z2, z2, oda2, x2, mkv3, w_s, b_s_t, wa, wsg, wm, wo, ln_g, ln_b)


def _ffn_kernel(x_ref, wi_ref, wo_ref, g_ref, b_ref, o_ref, *, d_ff, tm):
    halves = [slice(h * MIX_HALF, (h + 1) * MIX_HALF) for h in range(tm // MIX_HALF)]
    hid = [_dot(x_ref[hs, :].astype(BF16), wi_ref[...]) for hs in halves]
    act = [((h[:, :d_ff] * _sigmoid(h[:, :d_ff])) * h[:, d_ff:]).astype(BF16) for h in hid]
    y = [_dot(a, wo_ref[...]) for a in act]
    for hi, hs in enumerate(halves):
        o_ref[hs, :] = _layer_norm(ALPHA * x_ref[hs, :] + y[hi], g_ref[...], b_ref[...])


def _ffn(x1, wi, wo, ln_g, ln_b, tm):
    n, d = x1.shape
    d_ff = wo.shape[0]
    full = lambda shape: pl.BlockSpec(shape, lambda i: (0,) * len(shape),
                                      pipeline_mode=pl.Buffered(1))
    return pl.pallas_call(
        functools.partial(_ffn_kernel, d_ff=d_ff, tm=tm),
        grid=(n // tm,),
        in_specs=[pl.BlockSpec((tm, d), lambda i: (i, 0)),
                  full(wi.shape), full(wo.shape), full((1, d)), full((1, d))],
        out_specs=pl.BlockSpec((tm, d), lambda i: (i, 0)),
        out_shape=jax.ShapeDtypeStruct((n, d), F32),
        compiler_params=pltpu.CompilerParams(
            dimension_semantics=("arbitrary",), vmem_limit_bytes=VMEM_LIMIT),
        name="ffn",
    )(x1, wi, wo, ln_g, ln_b)


def kernel(x, mem, positions, w_in, lambda_q1, lambda_k1, lambda_q2, lambda_k2, da_subln_g,
           sg_norm_g, sg_norm_b, sg_w_s, sg_b_s, w_mem_kv, w_br_attn, w_br_sg, w_br_mem, w_out,
           ln1_g, ln1_b, w_ffn_in, w_ffn_out, ln2_g, ln2_b):
    bsz, seq, d = x.shape
    n = bsz * seq
    depth = w_in.shape[0]
    assert d == D_MODEL and depth == DEPTH

    inv_freq = ROPE_THETA ** (-jnp.arange(ROT_HALF, dtype=F32) * 2.0 / ROT_DIM)
    lane_d = jnp.arange(LANES) % DA_HEAD_DIM
    invf = jnp.where(lane_d < ROT_DIM, inv_freq[lane_d % ROT_HALF], 0.0).reshape(1, LANES)
    pos2 = positions.reshape(n, 1)

    x2 = x.reshape(n, d)
    for l in range(depth):
        lambda_init = 0.8 - 0.6 * math.exp(-0.3 * l)
        row = lambda a: a[l].reshape(1, -1).astype(F32)
        z2 = _proj_in(x2, w_in[l].astype(BF16), pos2, invf, row(sg_norm_g), row(sg_norm_b),
                      tm=PROJ_ROWS)
        oda = _diff_attn(z2.reshape(bsz, seq, -1), row(lambda_q1), row(lambda_k1),
                         row(lambda_q2), row(lambda_k2), row(da_subln_g), tq=ATTN_Q_ROWS, tk=ATTN_K_ROWS,
                         lambda_init=lambda_init)
        mkv = _mem_kv(mem.reshape(-1, d), w_mem_kv[l].astype(BF16)).reshape(bsz, mem.shape[1], -1)
        x1 = _mix(z2, oda.reshape(n, -1), x2, mkv, sg_w_s[l], sg_b_s[l].T,
                  w_br_attn[l].astype(BF16), w_br_sg[l].astype(BF16), w_br_mem[l].astype(BF16),
                  w_out[l].astype(BF16), row(ln1_g), row(ln1_b), tm=MIX_ROWS, seq=seq)
        x2 = _ffn(x1, w_ffn_in[l].astype(BF16), w_ffn_out[l].astype(BF16), row(ln2_g), row(ln2_b),
                  tm=FFN_ROWS)
    return x2.reshape(bsz, seq, d)
```

```python
import functools
import math

import jax
import jax.numpy as jnp
from jax import lax
from jax.experimental import pallas as pl
from jax.experimental.pallas import tpu as pltpu

D_MODEL = 1024
DA_HEADS = 8
DA_HEAD_DIM = 64
DA_V_DIM = 2 * DA_HEAD_DIM
ROPE_THETA = 500000.0
ROT_DIM = DA_HEAD_DIM // 4
ROT_HALF = ROT_DIM // 2
SG_GROUPS = 8
SG_CHUNK = 128
SG_GROUP_DIM = 128
XA_HEADS = 4
XA_HEAD_DIM = 256
N_SECTIONS = 9
SECTIONS_PER_GROUP = 3
DEPTH = 1
ALPHA = (2 * DEPTH) ** 0.25
LN_EPS = 1e-5
RMS_EPS = 1e-5
LANES = 128
COL_CHUNK = 256
MIX_HALF = 256

V7X_VMEM_BYTES = 64 * 1024 * 1024
VMEM_LIMIT = V7X_VMEM_BYTES * 7 // 8

PROJ_ROWS = 1024
ATTN_Q_ROWS = 512
ATTN_K_ROWS = 1024
MIX_ROWS = 512
FFN_ROWS = 512

F32 = jnp.float32
BF16 = jnp.bfloat16


def _dot(a, b):
    return jnp.dot(a, b, preferred_element_type=F32)


def _dot_nt(a, b):
    return lax.dot_general(a, b, (((1,), (1,)), ((), ())), preferred_element_type=F32)


def _sigmoid(x):
    return 0.5 + 0.5 * jnp.tanh(0.5 * x)


def _layer_norm(x, g, b):
    mu = jnp.mean(x, axis=-1, keepdims=True)
    xc = x - mu
    var = jnp.mean(xc * xc, axis=-1, keepdims=True)
    return xc * lax.rsqrt(var + LN_EPS) * g + b


def _proj_in_kernel(x_ref, w_ref, pos_ref, invf_ref, sgg_ref, sgb_ref, o_ref,
                    xb_ref, cos_ref, sin_ref, gel_ref, acc_ref):
    grp = pl.program_id(1)
    nchunk = D_MODEL // COL_CHUNK
    pending = []
    issued = [0]

    def flush():
        while pending:
            pending.pop(0)()

    def chunk(sec, c, epilogue):
        cs = slice(sec * D_MODEL + c * COL_CHUNK, sec * D_MODEL + (c + 1) * COL_CHUNK)
        slot = issued[0] % nchunk
        issued[0] += 1
        park = slice(slot * COL_CHUNK, (slot + 1) * COL_CHUNK)
        acc_ref[:, park] = _dot(xb_ref[...], w_ref[:, cs])
        flush()
        pending.append(lambda: epilogue(acc_ref[:, park], cs))

    def run(sec, epilogue):
        for c in range(nchunk):
            chunk(sec, c, epilogue)

    def plain(acc, cs):
        o_ref[:, cs] = acc.astype(BF16)

    def rope(scale):
        d = lax.broadcasted_iota(jnp.int32, (1, LANES), 1) % DA_HEAD_DIM
        sgn_lo = jnp.where(d < ROT_HALF, -scale, 0.0)
        sgn_hi = jnp.where((d >= ROT_HALF) & (d < ROT_DIM), scale, 0.0)

        def epilogue(acc, cs):
            cos = cos_ref[...] * scale
            sin_lo = sin_ref[...] * sgn_lo
            sin_hi = sin_ref[...] * sgn_hi
            for h in range(COL_CHUNK // LANES):
                t = acc[:, h * LANES:(h + 1) * LANES]
                t_up = pltpu.roll(t, LANES - ROT_HALF, 1)
                t_dn = pltpu.roll(t, ROT_HALF, 1)
                r = t * cos + t_up * sin_lo + t_dn * sin_hi
                o_ref[:, cs.start + h * LANES:cs.start + (h + 1) * LANES] = r.astype(BF16)
        return epilogue

    @pl.when(grp == 0)
    def _():
        xb_ref[...] = x_ref[...].astype(BF16)
        ang = pos_ref[...].astype(F32) * invf_ref[...]
        cos_ref[...] = jnp.cos(ang)
        sin_ref[...] = jnp.sin(ang)
        run(0, rope(DA_HEAD_DIM ** -0.5 * math.log2(math.e)))
        run(1, rope(1.0))
        run(2, plain)
        flush()

    @pl.when(grp == 1)
    def _():
        def gelu_keep(acc, cs):
            gel_ref[:, cs.start - D_MODEL:cs.stop - D_MODEL] = jax.nn.gelu(acc)

        def gelu_out(acc, cs):
            o_ref[:, cs] = jax.nn.gelu(acc).astype(BF16)

        def norm_sv():
            o_ref[:, D_MODEL:2 * D_MODEL] = _layer_norm(
                gel_ref[...], sgg_ref[...], sgb_ref[...]).astype(BF16)

        run(1, gelu_keep)
        pending.append(norm_sv)
        run(0, gelu_out)
        run(2, plain)
        flush()

    @pl.when(grp == 2)
    def _():
        def gate(acc, cs):
            o_ref[:, cs] = _sigmoid(acc).astype(BF16)
        for sec in range(SECTIONS_PER_GROUP):
            run(sec, gate)
        flush()


def _proj_in(x2, w_in_b, pos2, invf, sgg, sgb, tm):
    n = x2.shape[0]
    gw = SECTIONS_PER_GROUP * D_MODEL
    return pl.pallas_call(
        _proj_in_kernel,
        grid=(n // tm, N_SECTIONS // SECTIONS_PER_GROUP),
        in_specs=[
            pl.BlockSpec((tm, D_MODEL), lambda i, j: (i, 0)),
            pl.BlockSpec((D_MODEL, gw), lambda i, j: (0, j)),
            pl.BlockSpec((tm, 1), lambda i, j: (i, 0)),
            pl.BlockSpec((1, LANES), lambda i, j: (0, 0)),
            pl.BlockSpec((1, D_MODEL), lambda i, j: (0, 0)),
            pl.BlockSpec((1, D_MODEL), lambda i, j: (0, 0)),
        ],
        out_specs=pl.BlockSpec((tm, gw), lambda i, j: (i, j)),
        out_shape=jax.ShapeDtypeStruct((n, N_SECTIONS * D_MODEL), BF16),
        scratch_shapes=[
            pltpu.VMEM((tm, D_MODEL), BF16),
            pltpu.VMEM((tm, LANES), F32),
            pltpu.VMEM((tm, LANES), F32),
            pltpu.VMEM((tm, D_MODEL), F32),
            pltpu.VMEM((tm, D_MODEL), F32),
        ],
        compiler_params=pltpu.CompilerParams(
            dimension_semantics=("arbitrary", "arbitrary"), vmem_limit_bytes=VMEM_LIMIT),
        name="proj_in",
    )(x2, w_in_b, pos2, invf, sgg, sgb)


def _diff_attn_kernel(q_ref, k_ref, v_ref, lq1_ref, lk1_ref, lq2_ref, lk2_ref, g_ref, o_ref,
                      qs_ref, s0_ref, s1_ref, mx0_ref, mx1_ref, m_ref, l_ref, acc_ref,
                      *, tq, tk, nq, lambda_init):
    s_ref = (s0_ref, s1_ref)
    mx_ref = (mx0_ref, mx1_ref)
    rows = 2 * tq
    lane = lax.broadcasted_iota(jnp.int32, (1, LANES), 1)

    def scores(jb):
        start = pl.multiple_of(jb * tk, tk)
        kb = k_ref[0, pl.ds(start, tk), :]
        s = _dot_nt(qs_ref[...], kb)
        smax = s[:, 0:LANES]
        for g in range(1, tk // LANES):
            smax = jnp.maximum(smax, s[:, g * LANES:(g + 1) * LANES])
        return s, smax

    def begin(qi):
        q = q_ref[0, pl.ds(pl.multiple_of(qi * tq, tq), tq), :]
        qs_ref[0:tq, :] = jnp.where(lane < DA_HEAD_DIM, q, jnp.zeros_like(q))
        qs_ref[tq:rows, :] = jnp.where(lane >= DA_HEAD_DIM, q, jnp.zeros_like(q))
        s_ref[0][...], mx_ref[0][...] = scores(0)
        m_ref[...] = jnp.full(m_ref.shape, -jnp.inf, F32)
        l_ref[...] = jnp.zeros(l_ref.shape, F32)
        acc_ref[...] = jnp.zeros(acc_ref.shape, F32)

    def finish(qi):
        lam = (jnp.exp(jnp.sum(lq1_ref[...] * lk1_ref[...], axis=-1, keepdims=True))
               - jnp.exp(jnp.sum(lq2_ref[...] * lk2_ref[...], axis=-1, keepdims=True))
               + lambda_init)
        o = (acc_ref[0:tq, :] / l_ref[0:tq, :]
             - lam * (acc_ref[tq:rows, :] / l_ref[tq:rows, :]))
        ms = jnp.mean(o * o, axis=-1, keepdims=True)
        o = o * lax.rsqrt(ms + RMS_EPS) * g_ref[...] * (1.0 - lambda_init)
        o_ref[0, pl.ds(pl.multiple_of(qi * tq, tq), tq), :] = o.astype(BF16)

    def softmax_pv(jb, slot, width, mask_from, smax=None):
        ngrp = width // LANES
        start = pl.multiple_of(jb * tk, tk)
        vb = v_ref[0, pl.ds(start, width), :]
        row = lax.broadcasted_iota(jnp.int32, (rows, LANES), 0) & (tq - 1)
        col = lax.broadcasted_iota(jnp.int32, (rows, LANES), 1)

        def sgrp(g):
            x = s_ref[slot][:, g * LANES:(g + 1) * LANES]
            if g >= mask_from:
                x = jnp.where(col + (g - mask_from) * LANES <= row, x, -jnp.inf)
            return x

        if smax is None:
            smax = sgrp(0)
            for g in range(1, ngrp):
                smax = jnp.maximum(smax, sgrp(g))
        m_old = m_ref[...]
        m_new = jnp.maximum(m_old, jnp.max(smax, axis=-1, keepdims=True))
        m_ref[...] = m_new
        a = jnp.exp2(m_old - m_new)
        p = jnp.concatenate([jnp.exp2((sgrp(g) - m_new).astype(BF16)) for g in range(ngrp)],
                            axis=1)
        v1 = jnp.concatenate([vb, jnp.ones((width, LANES), BF16)], axis=1)
        pv = _dot(p, v1)
        acc_ref[...] = a * acc_ref[...] + pv[:, :DA_V_DIM]
        l_ref[...] = a * l_ref[...] + pv[:, DA_V_DIM:]

    def step(jb, slot):
        s_ref[1 - slot][...], mx_ref[1 - slot][...] = scores(jb + 1)
        softmax_pv(jb, slot, tk, tk // LANES, smax=mx_ref[slot][...])

    def pair(p, carry):
        step(2 * p, 0)
        step(2 * p + 1, 1)
        return carry

    def query_block(i, carry):
        if tk == tq:
            nfull = i
        else:
            nfull = lax.shift_right_logical(i, 1)
        lax.fori_loop(0, lax.shift_right_logical(nfull, 1), pair, 0)

        def last_blocks(slot, width, mask_from):
            if slot == 1:
                step(nfull - 1, 0)
            softmax_pv(nfull, slot, width, mask_from)

        for slot in range(2):
            if tk == tq:
                pl.when((nfull & 1) == slot)(functools.partial(last_blocks, slot, tq, 0))
            else:
                pl.when(((i & 1) == 0) & ((nfull & 1) == slot))(
                    functools.partial(last_blocks, slot, tq, 0))
                pl.when(((i & 1) == 1) & ((nfull & 1) == slot))(
                    functools.partial(last_blocks, slot, tk, tq // LANES))

        @pl.when(i + 1 < nq)
        def _():
            finish(i)
            begin(i + 1)

        @pl.when(i + 1 == nq)
        def _():
            finish(i)

        return carry

    begin(0)
    lax.fori_loop(0, nq, query_block, 0)


def _diff_attn(z3, lq1, lk1, lq2, lk2, subln_g, tq, tk, lambda_init):
    bsz, seq, _ = z3.shape
    assert tk in (tq, 2 * tq)
    kern = functools.partial(_diff_attn_kernel, tq=tq, tk=tk, nq=seq // tq,
                             lambda_init=lambda_init)
    vec = lambda w: pl.BlockSpec((1, w), lambda b, h: (0, 0))
    return pl.pallas_call(
        kern,
        grid=(bsz, DA_HEADS),
        in_specs=[
            pl.BlockSpec((1, seq, LANES), lambda b, h: (b, 0, h)),
            pl.BlockSpec((1, seq, LANES), lambda b, h: (b, 0, DA_HEADS + h)),
            pl.BlockSpec((1, seq, LANES), lambda b, h: (b, 0, 2 * DA_HEADS + h)),
            vec(DA_HEAD_DIM), vec(DA_HEAD_DIM), vec(DA_HEAD_DIM), vec(DA_HEAD_DIM),
            vec(DA_V_DIM),
        ],
        out_specs=pl.BlockSpec((1, seq, LANES), lambda b, h: (b, 0, h)),
        out_shape=jax.ShapeDtypeStruct((bsz, seq, DA_HEADS * DA_V_DIM), BF16),
        scratch_shapes=[
            pltpu.VMEM((2 * tq, LANES), BF16),
            pltpu.VMEM((2 * tq, tk), F32),
            pltpu.VMEM((2 * tq, tk), F32),
            pltpu.VMEM((2 * tq, LANES), F32),
            pltpu.VMEM((2 * tq, LANES), F32),
            pltpu.VMEM((2 * tq, LANES), F32),
            pltpu.VMEM((2 * tq, LANES), F32),
            pltpu.VMEM((2 * tq, DA_V_DIM), F32),
        ],
        compiler_params=pltpu.CompilerParams(
            dimension_semantics=("arbitrary", "arbitrary"),
            vmem_limit_bytes=VMEM_LIMIT),
        name="diff_attn",
    )(z3, z3, z3, lq1, lk1, lq2, lk2, subln_g)


def _mem_kv_kernel(m_ref, w_ref, o_ref):
    o_ref[...] = _dot(m_ref[...].astype(BF16), w_ref[...]).astype(BF16)


def _mem_kv(mem2, w_b):
    rows, width = mem2.shape[0], w_b.shape[1]
    tn = 1024
    return pl.pallas_call(
        _mem_kv_kernel,
        grid=(width // tn,),
        in_specs=[pl.BlockSpec((rows, D_MODEL), lambda j: (0, 0)),
                  pl.BlockSpec((D_MODEL, tn), lambda j: (0, j))],
        out_specs=pl.BlockSpec((rows, tn), lambda j: (0, j)),
        out_shape=jax.ShapeDtypeStruct((rows, width), BF16),
        compiler_params=pltpu.CompilerParams(
            dimension_semantics=("arbitrary",), vmem_limit_bytes=VMEM_LIMIT),
        name="mem_kv",
    )(mem2, w_b)


def _mix_kernel(u_ref, v_ref, xq_ref, gate_ref, oda_ref, x_ref, mkv_ref, ws_ref, bs_ref,
                wa_ref, wsg_ref, wm_ref, wo_ref, g1_ref, b1_ref, o_ref, osg_ref, oxa_ref, *, tm):
    halves = [slice(h * MIX_HALF, (h + 1) * MIX_HALF) for h in range(tm // MIX_HALF)]
    d = D_MODEL
    row = lax.broadcasted_iota(jnp.int32, (SG_CHUNK, SG_CHUNK), 0)
    col = lax.broadcasted_iota(jnp.int32, (SG_CHUNK, SG_CHUNK), 1)
    ws = [jnp.where(col <= row, ws_ref[g], 0.0).astype(BF16) for g in range(SG_GROUPS)]

    def chunks(hs):
        return [slice(r, r + SG_CHUNK) for r in range(hs.start, hs.stop, SG_CHUNK)]

    def xa_cols(hd):
        return slice(hd * XA_HEAD_DIM, (hd + 1) * XA_HEAD_DIM)

    sg = {}
    for hs in halves:
        for g in range(SG_GROUPS):
            cs = slice(g * SG_GROUP_DIM, (g + 1) * SG_GROUP_DIM)
            for rs in chunks(hs):
                sg[rs.start, g] = _dot(ws[g], v_ref[rs, cs]) + bs_ref[:, g:g + 1]
    t_attn = [_dot(oda_ref[hs, :], wa_ref[...]) for hs in halves]
    xs = [[_dot_nt(xq_ref[hs, xa_cols(hd)], mkv_ref[0, :, xa_cols(hd)]) * (XA_HEAD_DIM ** -0.5)
           for hd in range(XA_HEADS)] for hs in halves]
    for hs in halves:
        for g in range(SG_GROUPS):
            cs = slice(g * SG_GROUP_DIM, (g + 1) * SG_GROUP_DIM)
            for rs in chunks(hs):
                osg_ref[rs, cs] = (u_ref[rs, cs].astype(F32) * sg[rs.start, g]).astype(BF16)
    t_sg = [_dot(osg_ref[hs, :], wsg_ref[...]) for hs in halves]
    for hi, hs in enumerate(halves):
        for hd in range(XA_HEADS):
            s = xs[hi][hd]
            p = jnp.exp(s - jnp.max(s, axis=-1, keepdims=True))
            p = p / jnp.sum(p, axis=-1, keepdims=True)
            vs = slice(XA_HEADS * XA_HEAD_DIM + hd * XA_HEAD_DIM,
                       XA_HEADS * XA_HEAD_DIM + (hd + 1) * XA_HEAD_DIM)
            oxa_ref[hs, xa_cols(hd)] = _dot(p.astype(BF16), mkv_ref[0, :, vs]).astype(BF16)
    t_mem = [_dot(oxa_ref[hs, :], wm_ref[...]) for hs in halves]
    merged = [(gate_ref[hs, 0:d].astype(F32) * t_attn[hi]
               + gate_ref[hs, d:2 * d].astype(F32) * t_sg[hi]
               + gate_ref[hs, 2 * d:3 * d].astype(F32) * t_mem[hi]).astype(BF16)
              for hi, hs in enumerate(halves)]
    y = [_dot(m, wo_ref[...]) for m in merged]
    for hi, hs in enumerate(halves):
        o_ref[hs, :] = _layer_norm(ALPHA * x_ref[hs, :] + y[hi], g1_ref[...], b1_ref[...])


def _mix(z2, oda2, x2, mkv3, w_s, b_s_t, wa, wsg, wm, wo, ln_g, ln_b, tm, seq):
    n = x2.shape[0]
    d = D_MODEL
    full = lambda shape: pl.BlockSpec(shape, lambda i: (0,) * len(shape))
    return pl.pallas_call(
        functools.partial(_mix_kernel, tm=tm),
        grid=(n // tm,),
        in_specs=[
            pl.BlockSpec((tm, d), lambda i: (i, 3)),
            pl.BlockSpec((tm, d), lambda i: (i, 4)),
            pl.BlockSpec((tm, d), lambda i: (i, 5)),
            pl.BlockSpec((tm, 3 * d), lambda i: (i, 2)),
            pl.BlockSpec((tm, d), lambda i: (i, 0)),
            pl.BlockSpec((tm, d), lambda i: (i, 0)),
            pl.BlockSpec((1, mkv3.shape[1], mkv3.shape[2]), lambda i: ((i * tm) // seq, 0, 0)),
            full(w_s.shape), full(b_s_t.shape),
            full((d, d)), full((d, d)), full((d, d)), full((d, d)),
            full((1, d)), full((1, d)),
        ],
        out_specs=pl.BlockSpec((tm, d), lambda i: (i, 0)),
        out_shape=jax.ShapeDtypeStruct((n, d), F32),
        scratch_shapes=[pltpu.VMEM((tm, d), BF16), pltpu.VMEM((tm, d), BF16)],
        compiler_params=pltpu.CompilerParams(
            dimension_semantics=("arbitrary",), vmem_limit_bytes=VMEM_LIMIT),
        name="mix",
    )(z2, z2, z2, z2, oda2, x2, mkv3, w_s, b_s_t, wa, wsg, wm, wo, ln_g, ln_b)


def _ffn_kernel(x_ref, wi_ref, wo_ref, g_ref, b_ref, o_ref, *, d_ff, tm):
    halves = [slice(h * MIX_HALF, (h + 1) * MIX_HALF) for h in range(tm // MIX_HALF)]
    hid = [_dot(x_ref[hs, :].astype(BF16), wi_ref[...]) for hs in halves]
    act = [((h[:, :d_ff] * _sigmoid(h[:, :d_ff])) * h[:, d_ff:]).astype(BF16) for h in hid]
    y = [_dot(a, wo_ref[...]) for a in act]
    for hi, hs in enumerate(halves):
        o_ref[hs, :] = _layer_norm(ALPHA * x_ref[hs, :] + y[hi], g_ref[...], b_ref[...])


def _ffn(x1, wi, wo, ln_g, ln_b, tm):
    n, d = x1.shape
    d_ff = wo.shape[0]
    full = lambda shape: pl.BlockSpec(shape, lambda i: (0,) * len(shape),
                                      pipeline_mode=pl.Buffered(1))
    return pl.pallas_call(
        functools.partial(_ffn_kernel, d_ff=d_ff, tm=tm),
        grid=(n // tm,),
        in_specs=[pl.BlockSpec((tm, d), lambda i: (i, 0)),
                  full(wi.shape), full(wo.shape), full((1, d)), full((1, d))],
        out_specs=pl.BlockSpec((tm, d), lambda i: (i, 0)),
        out_shape=jax.ShapeDtypeStruct((n, d), F32),
        compiler_params=pltpu.CompilerParams(
            dimension_semantics=("arbitrary",), vmem_limit_bytes=VMEM_LIMIT),
        name="ffn",
    )(x1, wi, wo, ln_g, ln_b)


def kernel(x, mem, positions, w_in, lambda_q1, lambda_k1, lambda_q2, lambda_k2, da_subln_g,
           sg_norm_g, sg_norm_b, sg_w_s, sg_b_s, w_mem_kv, w_br_attn, w_br_sg, w_br_mem, w_out,
           ln1_g, ln1_b, w_ffn_in, w_ffn_out, ln2_g, ln2_b):
    bsz, seq, d = x.shape
    n = bsz * seq
    depth = w_in.shape[0]
    assert d == D_MODEL and depth == DEPTH

    inv_freq = ROPE_THETA ** (-jnp.arange(ROT_HALF, dtype=F32) * 2.0 / ROT_DIM)
    lane_d = jnp.arange(LANES) % DA_HEAD_DIM
    invf = jnp.where(lane_d < ROT_DIM, inv_freq[lane_d % ROT_HALF], 0.0).reshape(1, LANES)
    pos2 = positions.reshape(n, 1)

    x2 = x.reshape(n, d)
    for l in range(depth):
        lambda_init = 0.8 - 0.6 * math.exp(-0.3 * l)
        row = lambda a: a[l].reshape(1, -1).astype(F32)
        z2 = _proj_in(x2, w_in[l].astype(BF16), pos2, invf, row(sg_norm_g), row(sg_norm_b),
                      tm=PROJ_ROWS)
        oda = _diff_attn(z2.reshape(bsz, seq, -1), row(lambda_q1), row(lambda_k1),
                         row(lambda_q2), row(lambda_k2), row(da_subln_g), tq=ATTN_Q_ROWS, tk=ATTN_K_ROWS,
                         lambda_init=lambda_init)
        mkv = _mem_kv(mem.reshape(-1, d), w_mem_kv[l].astype(BF16)).reshape(bsz, mem.shape[1], -1)
        x1 = _mix(z2, oda.reshape(n, -1), x2, mkv, sg_w_s[l], sg_b_s[l].T,
                  w_br_attn[l].astype(BF16), w_br_sg[l].astype(BF16), w_br_mem[l].astype(BF16),
                  w_out[l].astype(BF16), row(ln1_g), row(ln1_b), tm=MIX_ROWS, seq=seq)
        x2 = _ffn(x1, w_ffn_in[l].astype(BF16), w_ffn_out[l].astype(BF16), row(ln2_g), row(ln2_b),
                  tm=FFN_ROWS)
    return x2.reshape(bsz, seq, d)
```

```python
import functools
import math

import jax
import jax.numpy as jnp
from jax import lax
from jax.experimental import pallas as pl
from jax.experimental.pallas import tpu as pltpu

D_MODEL = 1024
DA_HEADS = 8
DA_HEAD_DIM = 64
DA_V_DIM = 2 * DA_HEAD_DIM
ROPE_THETA = 500000.0
ROT_DIM = DA_HEAD_DIM // 4
ROT_HALF = ROT_DIM // 2
SG_GROUPS = 8
SG_CHUNK = 128
SG_GROUP_DIM = 128
XA_HEADS = 4
XA_HEAD_DIM = 256
N_SECTIONS = 9
SECTIONS_PER_GROUP = 3
DEPTH = 1
ALPHA = (2 * DEPTH) ** 0.25
LN_EPS = 1e-5
RMS_EPS = 1e-5
LANES = 128
COL_CHUNK = 256
MIX_HALF = 256

V7X_VMEM_BYTES = 64 * 1024 * 1024
VMEM_LIMIT = V7X_VMEM_BYTES * 7 // 8

PROJ_ROWS = 1024
ATTN_Q_ROWS = 512
ATTN_K_ROWS = 1024
MIX_ROWS = 512
FFN_ROWS = 512

F32 = jnp.float32
BF16 = jnp.bfloat16


def _dot(a, b):
    return jnp.dot(a, b, preferred_element_type=F32)


def _dot_nt(a, b):
    return lax.dot_general(a, b, (((1,), (1,)), ((), ())), preferred_element_type=F32)


def _sigmoid(x):
    return 0.5 + 0.5 * jnp.tanh(0.5 * x)


def _layer_norm(x, g, b):
    mu = jnp.mean(x, axis=-1, keepdims=True)
    xc = x - mu
    var = jnp.mean(xc * xc, axis=-1, keepdims=True)
    return xc * lax.rsqrt(var + LN_EPS) * g + b


def _proj_in_kernel(x_ref, w_ref, pos_ref, invf_ref, sgg_ref, sgb_ref, o_ref,
                    xb_ref, cos_ref, sin_ref, gel_ref, acc_ref):
    grp = pl.program_id(1)
    nchunk = D_MODEL // COL_CHUNK
    pending = []
    issued = [0]

    def flush():
        while pending:
            pending.pop(0)()

    def chunk(sec, c, epilogue):
        cs = slice(sec * D_MODEL + c * COL_CHUNK, sec * D_MODEL + (c + 1) * COL_CHUNK)
        slot = issued[0] % nchunk
        issued[0] += 1
        park = slice(slot * COL_CHUNK, (slot + 1) * COL_CHUNK)
        acc_ref[:, park] = _dot(xb_ref[...], w_ref[:, cs])
        flush()
        pending.append(lambda: epilogue(acc_ref[:, park], cs))

    def run(sec, epilogue):
        for c in range(nchunk):
            chunk(sec, c, epilogue)

    def plain(acc, cs):
        o_ref[:, cs] = acc.astype(BF16)

    def rope(scale):
        d = lax.broadcasted_iota(jnp.int32, (1, LANES), 1) % DA_HEAD_DIM
        sgn_lo = jnp.where(d < ROT_HALF, -scale, 0.0)
        sgn_hi = jnp.where((d >= ROT_HALF) & (d < ROT_DIM), scale, 0.0)

        def epilogue(acc, cs):
            cos = cos_ref[...] * scale
            sin_lo = sin_ref[...] * sgn_lo
            sin_hi = sin_ref[...] * sgn_hi
            for h in range(COL_CHUNK // LANES):
                t = acc[:, h * LANES:(h + 1) * LANES]
                t_up = pltpu.roll(t, LANES - ROT_HALF, 1)
                t_dn = pltpu.roll(t, ROT_HALF, 1)
                r = t * cos + t_up * sin_lo + t_dn * sin_hi
                o_ref[:, cs.start + h * LANES:cs.start + (h + 1) * LANES] = r.astype(BF16)
        return epilogue

    @pl.when(grp == 0)
    def _():
        xb_ref[...] = x_ref[...].astype(BF16)
        ang = invf_ref[...] * pos_ref[0].astype(F32)
        reps = LANES // ROT_HALF
        lane = lax.broadcasted_iota(jnp.int32, (1, LANES), 1)
        rotary = (lane % DA_HEAD_DIM) < ROT_DIM
        cos_ref[...] = jnp.where(rotary, jnp.tile(jnp.cos(ang), (reps, 1)).T, 1.0)
        sin_ref[...] = jnp.where(rotary, jnp.tile(jnp.sin(ang), (reps, 1)).T, 0.0)
        run(0, rope(DA_HEAD_DIM ** -0.5 * math.log2(math.e)))
        run(1, rope(1.0))
        run(2, plain)
        flush()

    @pl.when(grp == 1)
    def _():
        def gelu_keep(acc, cs):
            gel_ref[:, cs.start - D_MODEL:cs.stop - D_MODEL] = jax.nn.gelu(acc)

        def gelu_out(acc, cs):
            o_ref[:, cs] = jax.nn.gelu(acc).astype(BF16)

        def norm_sv():
            o_ref[:, D_MODEL:2 * D_MODEL] = _layer_norm(
                gel_ref[...], sgg_ref[...], sgb_ref[...]).astype(BF16)

        run(1, gelu_keep)
        pending.append(norm_sv)
        run(0, gelu_out)
        run(2, plain)
        flush()

    @pl.when(grp == 2)
    def _():
        def gate(acc, cs):
            o_ref[:, cs] = _sigmoid(acc).astype(BF16)
        for sec in range(SECTIONS_PER_GROUP):
            run(sec, gate)
        flush()


def _proj_in(x2, w_in_b, pos2, invf, sgg, sgb, tm):
    n = x2.shape[0]
    gw = SECTIONS_PER_GROUP * D_MODEL
    return pl.pallas_call(
        _proj_in_kernel,
        grid=(n // tm, N_SECTIONS // SECTIONS_PER_GROUP),
        in_specs=[
            pl.BlockSpec((tm, D_MODEL), lambda i, j: (i, 0)),
            pl.BlockSpec((D_MODEL, gw), lambda i, j: (0, j)),
            pl.BlockSpec((1, 1, tm), lambda i, j: (i, 0, 0)),
            pl.BlockSpec((ROT_HALF, 1), lambda i, j: (0, 0)),
            pl.BlockSpec((1, D_MODEL), lambda i, j: (0, 0)),
            pl.BlockSpec((1, D_MODEL), lambda i, j: (0, 0)),
        ],
        out_specs=pl.BlockSpec((tm, gw), lambda i, j: (i, j)),
        out_shape=jax.ShapeDtypeStruct((n, N_SECTIONS * D_MODEL), BF16),
        scratch_shapes=[
            pltpu.VMEM((tm, D_MODEL), BF16),
            pltpu.VMEM((tm, LANES), F32),
            pltpu.VMEM((tm, LANES), F32),
            pltpu.VMEM((tm, D_MODEL), F32),
            pltpu.VMEM((tm, D_MODEL), F32),
        ],
        compiler_params=pltpu.CompilerParams(
            dimension_semantics=("arbitrary", "arbitrary"), vmem_limit_bytes=VMEM_LIMIT),
        name="proj_in",
    )(x2, w_in_b, pos2, invf, sgg, sgb)


def _diff_attn_kernel(q_ref, k_ref, v_ref, lq1_ref, lk1_ref, lq2_ref, lk2_ref, g_ref, o_ref,
                      qs_ref, s0_ref, s1_ref, mx0_ref, mx1_ref, m_ref, l_ref, acc_ref,
                      *, tq, tk, nq, lambda_init):
    s_ref = (s0_ref, s1_ref)
    mx_ref = (mx0_ref, mx1_ref)
    rows = 2 * tq
    lane = lax.broadcasted_iota(jnp.int32, (1, LANES), 1)

    def scores(jb):
        start = pl.multiple_of(jb * tk, tk)
        kb = k_ref[0, pl.ds(start, tk), :]
        s = _dot_nt(qs_ref[...], kb)
        smax = s[:, 0:LANES]
        for g in range(1, tk // LANES):
            smax = jnp.maximum(smax, s[:, g * LANES:(g + 1) * LANES])
        return s, smax

    def begin(qi):
        q = q_ref[0, pl.ds(pl.multiple_of(qi * tq, tq), tq), :]
        qs_ref[0:tq, :] = jnp.where(lane < DA_HEAD_DIM, q, jnp.zeros_like(q))
        qs_ref[tq:rows, :] = jnp.where(lane >= DA_HEAD_DIM, q, jnp.zeros_like(q))
        s_ref[0][...], mx_ref[0][...] = scores(0)
        m_ref[...] = jnp.full(m_ref.shape, -jnp.inf, F32)
        l_ref[...] = jnp.zeros(l_ref.shape, F32)
        acc_ref[...] = jnp.zeros(acc_ref.shape, F32)

    def finish(qi):
        lam = (jnp.exp(jnp.sum(lq1_ref[...] * lk1_ref[...], axis=-1, keepdims=True))
               - jnp.exp(jnp.sum(lq2_ref[...] * lk2_ref[...], axis=-1, keepdims=True))
               + lambda_init)
        o = (acc_ref[0:tq, :] / l_ref[0:tq, :]
             - lam * (acc_ref[tq:rows, :] / l_ref[tq:rows, :]))
        ms = jnp.mean(o * o, axis=-1, keepdims=True)
        o = o * lax.rsqrt(ms + RMS_EPS) * g_ref[...] * (1.0 - lambda_init)
        o_ref[0, pl.ds(pl.multiple_of(qi * tq, tq), tq), :] = o.astype(BF16)

    def softmax_pv(jb, slot, width, mask_from, smax=None):
        ngrp = width // LANES
        start = pl.multiple_of(jb * tk, tk)
        vb = v_ref[0, pl.ds(start, width), :]
        row = lax.broadcasted_iota(jnp.int32, (rows, LANES), 0) & (tq - 1)
        col = lax.broadcasted_iota(jnp.int32, (rows, LANES), 1)

        def sgrp(g):
            x = s_ref[slot][:, g * LANES:(g + 1) * LANES]
            if g >= mask_from:
                x = jnp.where(col + (g - mask_from) * LANES <= row, x, -jnp.inf)
            return x

        if smax is None:
            smax = sgrp(0)
            for g in range(1, ngrp):
                smax = jnp.maximum(smax, sgrp(g))
        m_old = m_ref[...]
        m_new = jnp.maximum(m_old, jnp.max(smax, axis=-1, keepdims=True))
        m_ref[...] = m_new
        a = jnp.exp2(m_old - m_new)
        p = jnp.concatenate([jnp.exp2((sgrp(g) - m_new).astype(BF16)) for g in range(ngrp)],
                            axis=1)
        v1 = jnp.concatenate([vb, jnp.ones((width, LANES), BF16)], axis=1)
        pv = _dot(p, v1)
        acc_ref[...] = a * acc_ref[...] + pv[:, :DA_V_DIM]
        l_ref[...] = a * l_ref[...] + pv[:, DA_V_DIM:]

    def step(jb, slot):
        s_ref[1 - slot][...], mx_ref[1 - slot][...] = scores(jb + 1)
        softmax_pv(jb, slot, tk, tk // LANES, smax=mx_ref[slot][...])

    def pair(p, carry):
        step(2 * p, 0)
        step(2 * p + 1, 1)
        return carry

    def query_block(i, carry):
        if tk == tq:
            nfull = i
        else:
            nfull = lax.shift_right_logical(i, 1)
        lax.fori_loop(0, lax.shift_right_logical(nfull, 1), pair, 0)

        def last_blocks(slot, width, mask_from):
            if slot == 1:
                step(nfull - 1, 0)
            softmax_pv(nfull, slot, width, mask_from)

        for slot in range(2):
            if tk == tq:
                pl.when((nfull & 1) == slot)(functools.partial(last_blocks, slot, tq, 0))
            else:
                pl.when(((i & 1) == 0) & ((nfull & 1) == slot))(
                    functools.partial(last_blocks, slot, tq, 0))
                pl.when(((i & 1) == 1) & ((nfull & 1) == slot))(
                    functools.partial(last_blocks, slot, tk, tq // LANES))

        @pl.when(i + 1 < nq)
        def _():
            finish(i)
            begin(i + 1)

        @pl.when(i + 1 == nq)
        def _():
            finish(i)

        return carry

    begin(0)
    lax.fori_loop(0, nq, query_block, 0)


def _diff_attn(z3, lq1, lk1, lq2, lk2, subln_g, tq, tk, lambda_init):
    bsz, seq, _ = z3.shape
    assert tk in (tq, 2 * tq)
    kern = functools.partial(_diff_attn_kernel, tq=tq, tk=tk, nq=seq // tq,
                             lambda_init=lambda_init)
    vec = lambda w: pl.BlockSpec((1, w), lambda b, h: (0, 0))
    return pl.pallas_call(
        kern,
        grid=(bsz, DA_HEADS),
        in_specs=[
            pl.BlockSpec((1, seq, LANES), lambda b, h: (b, 0, h)),
            pl.BlockSpec((1, seq, LANES), lambda b, h: (b, 0, DA_HEADS + h)),
            pl.BlockSpec((1, seq, LANES), lambda b, h: (b, 0, 2 * DA_HEADS + h)),
            vec(DA_HEAD_DIM), vec(DA_HEAD_DIM), vec(DA_HEAD_DIM), vec(DA_HEAD_DIM),
            vec(DA_V_DIM),
        ],
        out_specs=pl.BlockSpec((1, seq, LANES), lambda b, h: (b, 0, h)),
        out_shape=jax.ShapeDtypeStruct((bsz, seq, DA_HEADS * DA_V_DIM), BF16),
        scratch_shapes=[
            pltpu.VMEM((2 * tq, LANES), BF16),
            pltpu.VMEM((2 * tq, tk), F32),
            pltpu.VMEM((2 * tq, tk), F32),
            pltpu.VMEM((2 * tq, LANES), F32),
            pltpu.VMEM((2 * tq, LANES), F32),
            pltpu.VMEM((2 * tq, LANES), F32),
            pltpu.VMEM((2 * tq, LANES), F32),
            pltpu.VMEM((2 * tq, DA_V_DIM), F32),
        ],
        compiler_params=pltpu.CompilerParams(
            dimension_semantics=("arbitrary", "arbitrary"),
            vmem_limit_bytes=VMEM_LIMIT),
        name="diff_attn",
    )(z3, z3, z3, lq1, lk1, lq2, lk2, subln_g)


def _mem_kv_kernel(m_ref, w_ref, o_ref):
    o_ref[...] = _dot(m_ref[...].astype(BF16), w_ref[...]).astype(BF16)


def _mem_kv(mem2, w_b):
    rows, width = mem2.shape[0], w_b.shape[1]
    tn = 1024
    return pl.pallas_call(
        _mem_kv_kernel,
        grid=(width // tn,),
        in_specs=[pl.BlockSpec((rows, D_MODEL), lambda j: (0, 0)),
                  pl.BlockSpec((D_MODEL, tn), lambda j: (0, j))],
        out_specs=pl.BlockSpec((rows, tn), lambda j: (0, j)),
        out_shape=jax.ShapeDtypeStruct((rows, width), BF16),
        compiler_params=pltpu.CompilerParams(
            dimension_semantics=("arbitrary",), vmem_limit_bytes=VMEM_LIMIT),
        name="mem_kv",
    )(mem2, w_b)


def _mix_kernel(u_ref, v_ref, xq_ref, gate_ref, oda_ref, x_ref, mkv_ref, ws_ref, bs_ref,
                wa_ref, wsg_ref, wm_ref, wo_ref, g1_ref, b1_ref, o_ref, osg_ref, oxa_ref, *, tm):
    halves = [slice(h * MIX_HALF, (h + 1) * MIX_HALF) for h in range(tm // MIX_HALF)]
    d = D_MODEL
    row = lax.broadcasted_iota(jnp.int32, (SG_CHUNK, SG_CHUNK), 0)
    col = lax.broadcasted_iota(jnp.int32, (SG_CHUNK, SG_CHUNK), 1)
    ws = [jnp.where(col <= row, ws_ref[g], 0.0).astype(BF16) for g in range(SG_GROUPS)]

    def chunks(hs):
        return [slice(r, r + SG_CHUNK) for r in range(hs.start, hs.stop, SG_CHUNK)]

    def xa_cols(hd):
        return slice(hd * XA_HEAD_DIM, (hd + 1) * XA_HEAD_DIM)

    sg = {}
    for hs in halves:
        for g in range(SG_GROUPS):
            cs = slice(g * SG_GROUP_DIM, (g + 1) * SG_GROUP_DIM)
            for rs in chunks(hs):
                sg[rs.start, g] = _dot(ws[g], v_ref[rs, cs]) + bs_ref[:, g:g + 1]
    xs = [[_dot_nt(xq_ref[hs, xa_cols(hd)], mkv_ref[0, :, xa_cols(hd)]) * (XA_HEAD_DIM ** -0.5)
           for hd in range(XA_HEADS)] for hs in halves]
    t_attn = [_dot(oda_ref[hs, :], wa_ref[...]) for hs in halves]
    for hs in halves:
        for g in range(SG_GROUPS):
            cs = slice(g * SG_GROUP_DIM, (g + 1) * SG_GROUP_DIM)
            for rs in chunks(hs):
                osg_ref[rs, cs] = (u_ref[rs, cs].astype(F32) * sg[rs.start, g]).astype(BF16)
    t_sg = [_dot(osg_ref[hs, :], wsg_ref[...]) for hs in halves]
    for hi, hs in enumerate(halves):
        for hd in range(XA_HEADS):
            s = xs[hi][hd]
            p = jnp.exp(s - jnp.max(s, axis=-1, keepdims=True))
            p = p / jnp.sum(p, axis=-1, keepdims=True)
            vs = slice(XA_HEADS * XA_HEAD_DIM + hd * XA_HEAD_DIM,
                       XA_HEADS * XA_HEAD_DIM + (hd + 1) * XA_HEAD_DIM)
            oxa_ref[hs, xa_cols(hd)] = _dot(p.astype(BF16), mkv_ref[0, :, vs]).astype(BF16)
    t_mem = [_dot(oxa_ref[hs, :], wm_ref[...]) for hs in halves]
    merged = [(gate_ref[hs, 0:d].astype(F32) * t_attn[hi]
               + gate_ref[hs, d:2 * d].astype(F32) * t_sg[hi]
               + gate_ref[hs, 2 * d:3 * d].astype(F32) * t_mem[hi]).astype(BF16)
              for hi, hs in enumerate(halves)]
    y = [_dot(m, wo_ref[...]) for m in merged]
    for hi, hs in enumerate(halves):
        o_ref[hs, :] = _layer_norm(ALPHA * x_ref[hs, :] + y[hi], g1_ref[...], b1_ref[...])


def _mix(z2, oda2, x2, mkv3, w_s, b_s_t, wa, wsg, wm, wo, ln_g, ln_b, tm, seq):
    n = x2.shape[0]
    d = D_MODEL
    full = lambda shape: pl.BlockSpec(shape, lambda i: (0,) * len(shape))
    return pl.pallas_call(
        functools.partial(_mix_kernel, tm=tm),
        grid=(n // tm,),
        in_specs=[
            pl.BlockSpec((tm, d), lambda i: (i, 3)),
            pl.BlockSpec((tm, d), lambda i: (i, 4)),
            pl.BlockSpec((tm, d), lambda i: (i, 5)),
            pl.BlockSpec((tm, 3 * d), lambda i: (i, 2)),
            pl.BlockSpec((tm, d), lambda i: (i, 0)),
            pl.BlockSpec((tm, d), lambda i: (i, 0)),
            pl.BlockSpec((1, mkv3.shape[1], mkv3.shape[2]), lambda i: ((i * tm) // seq, 0, 0)),
            full(w_s.shape), full(b_s_t.shape),
            full((d, d)), full((d, d)), full((d, d)), full((d, d)),
            full((1, d)), full((1, d)),
        ],
        out_specs=pl.BlockSpec((tm, d), lambda i: (i, 0)),
        out_shape=jax.ShapeDtypeStruct((n, d), F32),
        scratch_shapes=[pltpu.VMEM((tm, d), BF16), pltpu.VMEM((tm, d), BF16)],
        compiler_params=pltpu.CompilerParams(
            dimension_semantics=("arbitrary",), vmem_limit_bytes=VMEM_LIMIT),
        name="mix",
    )(z2, z2, z2, z2, oda2, x2, mkv3, w_s, b_s_t, wa, wsg, wm, wo, ln_g, ln_b)


def _ffn_kernel(x_ref, wi_ref, wo_ref, g_ref, b_ref, o_ref, *, d_ff, tm):
    halves = [slice(h * MIX_HALF, (h + 1) * MIX_HALF) for h in range(tm // MIX_HALF)]
    hid = [_dot(x_ref[hs, :].astype(BF16), wi_ref[...]) for hs in halves]
    act = [((h[:, :d_ff] * _sigmoid(h[:, :d_ff])) * h[:, d_ff:]).astype(BF16) for h in hid]
    y = [_dot(a, wo_ref[...]) for a in act]
    for hi, hs in enumerate(halves):
        o_ref[hs, :] = _layer_norm(ALPHA * x_ref[hs, :] + y[hi], g_ref[...], b_ref[...])


def _ffn(x1, wi, wo, ln_g, ln_b, tm):
    n, d = x1.shape
    d_ff = wo.shape[0]
    full = lambda shape: pl.BlockSpec(shape, lambda i: (0,) * len(shape),
                                      pipeline_mode=pl.Buffered(1))
    return pl.pallas_call(
        functools.partial(_ffn_kernel, d_ff=d_ff, tm=tm),
        grid=(n // tm,),
        in_specs=[pl.BlockSpec((tm, d), lambda i: (i, 0)),
                  full(wi.shape), full(wo.shape), full((1, d)), full((1, d))],
        out_specs=pl.BlockSpec((tm, d), lambda i: (i, 0)),
        out_shape=jax.ShapeDtypeStruct((n, d), F32),
        compiler_params=pltpu.CompilerParams(
            dimension_semantics=("arbitrary",), vmem_limit_bytes=VMEM_LIMIT),
        name="ffn",
    )(x1, wi, wo, ln_g, ln_b)


def kernel(x, mem, positions, w_in, lambda_q1, lambda_k1, lambda_q2, lambda_k2, da_subln_g,
           sg_norm_g, sg_norm_b, sg_w_s, sg_b_s, w_mem_kv, w_br_attn, w_br_sg, w_br_mem, w_out,
           ln1_g, ln1_b, w_ffn_in, w_ffn_out, ln2_g, ln2_b):
    bsz, seq, d = x.shape
    n = bsz * seq
    depth = w_in.shape[0]
    assert d == D_MODEL and depth == DEPTH

    inv_freq = ROPE_THETA ** (-jnp.arange(ROT_HALF, dtype=F32) * 2.0 / ROT_DIM)
    invf = inv_freq.reshape(ROT_HALF, 1)
    pos2 = positions.reshape(n // PROJ_ROWS, 1, PROJ_ROWS)

    x2 = x.reshape(n, d)
    for l in range(depth):
        lambda_init = 0.8 - 0.6 * math.exp(-0.3 * l)
        row = lambda a: a[l].reshape(1, -1).astype(F32)
        z2 = _proj_in(x2, w_in[l].astype(BF16), pos2, invf, row(sg_norm_g), row(sg_norm_b),
                      tm=PROJ_ROWS)
        oda = _diff_attn(z2.reshape(bsz, seq, -1), row(lambda_q1), row(lambda_k1),
                         row(lambda_q2), row(lambda_k2), row(da_subln_g), tq=ATTN_Q_ROWS, tk=ATTN_K_ROWS,
                         lambda_init=lambda_init)
        mkv = _mem_kv(mem.reshape(-1, d), w_mem_kv[l].astype(BF16)).reshape(bsz, mem.shape[1], -1)
        x1 = _mix(z2, oda.reshape(n, -1), x2, mkv, sg_w_s[l], sg_b_s[l].T,
                  w_br_attn[l].astype(BF16), w_br_sg[l].astype(BF16), w_br_mem[l].astype(BF16),
                  w_out[l].astype(BF16), row(ln1_g), row(ln1_b), tm=MIX_ROWS, seq=seq)
        x2 = _ffn(x1, w_ffn_in[l].astype(BF16), w_ffn_out[l].astype(BF16), row(ln2_g), row(ln2_b),
                  tm=FFN_ROWS)
    return x2.reshape(bsz, seq, d)
```

```python
import functools
import math

import jax
import jax.numpy as jnp
from jax import lax
from jax.experimental import pallas as pl
from jax.experimental.pallas import tpu as pltpu

D_MODEL = 1024
DA_HEADS = 8
DA_HEAD_DIM = 64
DA_V_DIM = 2 * DA_HEAD_DIM
ROPE_THETA = 500000.0
ROT_DIM = DA_HEAD_DIM // 4
ROT_HALF = ROT_DIM // 2
SG_GROUPS = 8
SG_CHUNK = 128
SG_GROUP_DIM = 128
XA_HEADS = 4
XA_HEAD_DIM = 256
N_SECTIONS = 9
SECTIONS_PER_GROUP = 3
DEPTH = 1
ALPHA = (2 * DEPTH) ** 0.25
LN_EPS = 1e-5
RMS_EPS = 1e-5
LANES = 128
COL_CHUNK = 256
MIX_HALF = 256

V7X_VMEM_BYTES = 64 * 1024 * 1024
VMEM_LIMIT = V7X_VMEM_BYTES * 7 // 8

PROJ_ROWS = 1024
ATTN_Q_ROWS = 512
ATTN_K_ROWS = 1024
MIX_ROWS = 512
FFN_ROWS = 512

F32 = jnp.float32
BF16 = jnp.bfloat16


def _dot(a, b):
    return jnp.dot(a, b, preferred_element_type=F32)


def _dot_nt(a, b):
    return lax.dot_general(a, b, (((1,), (1,)), ((), ())), preferred_element_type=F32)


def _sigmoid(x):
    return 0.5 + 0.5 * jnp.tanh(0.5 * x)


def _gelu_tanh(x):
    c = math.sqrt(2.0 / math.pi)
    return x * (0.5 + 0.5 * jnp.tanh(x * (c + (0.044715 * c) * (x * x))))


def _layer_norm(x, g, b):
    mu = jnp.mean(x, axis=-1, keepdims=True)
    xc = x - mu
    var = jnp.mean(xc * xc, axis=-1, keepdims=True)
    return xc * lax.rsqrt(var + LN_EPS) * g + b


def _proj_in_kernel(x_ref, w_ref, pos_ref, invf_ref, sgg_ref, sgb_ref, o_ref,
                    xb_ref, cos_ref, sin_ref, gel_ref, acc_ref):
    grp = pl.program_id(1)
    nchunk = D_MODEL // COL_CHUNK
    pending = []
    issued = [0]

    def flush():
        while pending:
            pending.pop(0)()

    def chunk(sec, c, epilogue):
        cs = slice(sec * D_MODEL + c * COL_CHUNK, sec * D_MODEL + (c + 1) * COL_CHUNK)
        slot = issued[0] % nchunk
        issued[0] += 1
        park = slice(slot * COL_CHUNK, (slot + 1) * COL_CHUNK)
        acc_ref[:, park] = _dot(xb_ref[...], w_ref[:, cs])
        flush()
        pending.append(lambda: epilogue(acc_ref[:, park], cs))

    def run(sec, epilogue):
        for c in range(nchunk):
            chunk(sec, c, epilogue)

    def plain(acc, cs):
        o_ref[:, cs] = acc.astype(BF16)

    def rope(scale):
        d = lax.broadcasted_iota(jnp.int32, (1, LANES), 1) % DA_HEAD_DIM
        sgn_lo = jnp.where(d < ROT_HALF, -scale, 0.0)
        sgn_hi = jnp.where((d >= ROT_HALF) & (d < ROT_DIM), scale, 0.0)

        def epilogue(acc, cs):
            cos = cos_ref[...] * scale
            sin_lo = sin_ref[...] * sgn_lo
            sin_hi = sin_ref[...] * sgn_hi
            for h in range(COL_CHUNK // LANES):
                t = acc[:, h * LANES:(h + 1) * LANES]
                t_up = pltpu.roll(t, LANES - ROT_HALF, 1)
                t_dn = pltpu.roll(t, ROT_HALF, 1)
                r = t * cos + t_up * sin_lo + t_dn * sin_hi
                o_ref[:, cs.start + h * LANES:cs.start + (h + 1) * LANES] = r.astype(BF16)
        return epilogue

    @pl.when(grp == 0)
    def _():
        xb_ref[...] = x_ref[...].astype(BF16)
        ang = invf_ref[...] * pos_ref[0].astype(F32)
        reps = LANES // ROT_HALF
        lane = lax.broadcasted_iota(jnp.int32, (1, LANES), 1)
        rotary = (lane % DA_HEAD_DIM) < ROT_DIM
        cos_ref[...] = jnp.where(rotary, jnp.tile(jnp.cos(ang), (reps, 1)).T, 1.0)
        sin_ref[...] = jnp.where(rotary, jnp.tile(jnp.sin(ang), (reps, 1)).T, 0.0)
        run(0, rope(DA_HEAD_DIM ** -0.5 * math.log2(math.e)))
        run(1, rope(1.0))
        run(2, plain)
        flush()

    @pl.when(grp == 1)
    def _():
        def gelu_keep(acc, cs):
            gel_ref[:, cs.start - D_MODEL:cs.stop - D_MODEL] = _gelu_tanh(acc)

        def gelu_out(acc, cs):
            o_ref[:, cs] = _gelu_tanh(acc).astype(BF16)

        def norm_sv():
            o_ref[:, D_MODEL:2 * D_MODEL] = _layer_norm(
                gel_ref[...], sgg_ref[...], sgb_ref[...]).astype(BF16)

        run(1, gelu_keep)
        pending.append(norm_sv)
        run(0, gelu_out)
        run(2, plain)
        flush()

    @pl.when(grp == 2)
    def _():
        def gate(acc, cs):
            o_ref[:, cs] = _sigmoid(acc).astype(BF16)
        for sec in range(SECTIONS_PER_GROUP):
            run(sec, gate)
        flush()


def _proj_in(x2, w_in_b, pos2, invf, sgg, sgb, tm):
    n = x2.shape[0]
    gw = SECTIONS_PER_GROUP * D_MODEL
    return pl.pallas_call(
        _proj_in_kernel,
        grid=(n // tm, N_SECTIONS // SECTIONS_PER_GROUP),
        in_specs=[
            pl.BlockSpec((tm, D_MODEL), lambda i, j: (i, 0)),
            pl.BlockSpec((D_MODEL, gw), lambda i, j: (0, j)),
            pl.BlockSpec((1, 1, tm), lambda i, j: (i, 0, 0)),
            pl.BlockSpec((ROT_HALF, 1), lambda i, j: (0, 0)),
            pl.BlockSpec((1, D_MODEL), lambda i, j: (0, 0)),
            pl.BlockSpec((1, D_MODEL), lambda i, j: (0, 0)),
        ],
        out_specs=pl.BlockSpec((tm, gw), lambda i, j: (i, j)),
        out_shape=jax.ShapeDtypeStruct((n, N_SECTIONS * D_MODEL), BF16),
        scratch_shapes=[
            pltpu.VMEM((tm, D_MODEL), BF16),
            pltpu.VMEM((tm, LANES), F32),
            pltpu.VMEM((tm, LANES), F32),
            pltpu.VMEM((tm, D_MODEL), F32),
            pltpu.VMEM((tm, D_MODEL), F32),
        ],
        compiler_params=pltpu.CompilerParams(
            dimension_semantics=("arbitrary", "arbitrary"), vmem_limit_bytes=VMEM_LIMIT),
        name="proj_in",
    )(x2, w_in_b, pos2, invf, sgg, sgb)


def _diff_attn_kernel(q_ref, k_ref, v_ref, lq1_ref, lk1_ref, lq2_ref, lk2_ref, g_ref, o_ref,
                      qs_ref, s0_ref, s1_ref, mx0_ref, mx1_ref, m_ref, l_ref, acc_ref,
                      *, tq, tk, nq, lambda_init):
    s_ref = (s0_ref, s1_ref)
    mx_ref = (mx0_ref, mx1_ref)
    rows = 2 * tq
    lane = lax.broadcasted_iota(jnp.int32, (1, LANES), 1)

    def scores(jb):
        start = pl.multiple_of(jb * tk, tk)
        kb = k_ref[0, pl.ds(start, tk), :]
        s = _dot_nt(qs_ref[...], kb)
        smax = s[:, 0:LANES]
        for g in range(1, tk // LANES):
            smax = jnp.maximum(smax, s[:, g * LANES:(g + 1) * LANES])
        return s, smax

    def begin(qi):
        q = q_ref[0, pl.ds(pl.multiple_of(qi * tq, tq), tq), :]
        qs_ref[0:tq, :] = jnp.where(lane < DA_HEAD_DIM, q, jnp.zeros_like(q))
        qs_ref[tq:rows, :] = jnp.where(lane >= DA_HEAD_DIM, q, jnp.zeros_like(q))
        s_ref[0][...], mx_ref[0][...] = scores(0)
        m_ref[...] = jnp.full(m_ref.shape, -jnp.inf, F32)
        l_ref[...] = jnp.zeros(l_ref.shape, F32)
        acc_ref[...] = jnp.zeros(acc_ref.shape, F32)

    def finish(qi):
        lam = (jnp.exp(jnp.sum(lq1_ref[...] * lk1_ref[...], axis=-1, keepdims=True))
               - jnp.exp(jnp.sum(lq2_ref[...] * lk2_ref[...], axis=-1, keepdims=True))
               + lambda_init)
        o = (acc_ref[0:tq, :] / l_ref[0:tq, :]
             - lam * (acc_ref[tq:rows, :] / l_ref[tq:rows, :]))
        ms = jnp.mean(o * o, axis=-1, keepdims=True)
        o = o * lax.rsqrt(ms + RMS_EPS) * g_ref[...] * (1.0 - lambda_init)
        o_ref[0, pl.ds(pl.multiple_of(qi * tq, tq), tq), :] = o.astype(BF16)

    def softmax_pv(jb, slot, width, mask_from, smax=None):
        ngrp = width // LANES
        start = pl.multiple_of(jb * tk, tk)
        vb = v_ref[0, pl.ds(start, width), :]
        row = lax.broadcasted_iota(jnp.int32, (rows, LANES), 0) & (tq - 1)
        col = lax.broadcasted_iota(jnp.int32, (rows, LANES), 1)

        def sgrp(g):
            x = s_ref[slot][:, g * LANES:(g + 1) * LANES]
            if g >= mask_from:
                x = jnp.where(col + (g - mask_from) * LANES <= row, x, -jnp.inf)
            return x

        if smax is None:
            smax = sgrp(0)
            for g in range(1, ngrp):
                smax = jnp.maximum(smax, sgrp(g))
        m_old = m_ref[...]
        m_new = jnp.maximum(m_old, jnp.max(smax, axis=-1, keepdims=True))
        m_ref[...] = m_new
        a = jnp.exp2(m_old - m_new)
        p = jnp.concatenate([jnp.exp2((sgrp(g) - m_new).astype(BF16)) for g in range(ngrp)],
                            axis=1)
        v1 = jnp.concatenate([vb, jnp.ones((width, LANES), BF16)], axis=1)
        pv = _dot(p, v1)
        acc_ref[...] = a * acc_ref[...] + pv[:, :DA_V_DIM]
        l_ref[...] = a * l_ref[...] + pv[:, DA_V_DIM:]

    def step(jb, slot):
        s_ref[1 - slot][...], mx_ref[1 - slot][...] = scores(jb + 1)
        softmax_pv(jb, slot, tk, tk // LANES, smax=mx_ref[slot][...])

    def pair(p, carry):
        step(2 * p, 0)
        step(2 * p + 1, 1)
        return carry

    def query_block(i, carry):
        if tk == tq:
            nfull = i
        else:
            nfull = lax.shift_right_logical(i, 1)
        lax.fori_loop(0, lax.shift_right_logical(nfull, 1), pair, 0)

        def last_blocks(slot, width, mask_from):
            if slot == 1:
                step(nfull - 1, 0)
            softmax_pv(nfull, slot, width, mask_from)

        for slot in range(2):
            if tk == tq:
                pl.when((nfull & 1) == slot)(functools.partial(last_blocks, slot, tq, 0))
            else:
                pl.when(((i & 1) == 0) & ((nfull & 1) == slot))(
                    functools.partial(last_blocks, slot, tq, 0))
                pl.when(((i & 1) == 1) & ((nfull & 1) == slot))(
                    functools.partial(last_blocks, slot, tk, tq // LANES))

        @pl.when(i + 1 < nq)
        def _():
            finish(i)
            begin(i + 1)

        @pl.when(i + 1 == nq)
        def _():
            finish(i)

        return carry

    begin(0)
    lax.fori_loop(0, nq, query_block, 0)


def _diff_attn(z3, lq1, lk1, lq2, lk2, subln_g, tq, tk, lambda_init):
    bsz, seq, _ = z3.shape
    assert tk in (tq, 2 * tq)
    kern = functools.partial(_diff_attn_kernel, tq=tq, tk=tk, nq=seq // tq,
                             lambda_init=lambda_init)
    vec = lambda w: pl.BlockSpec((1, w), lambda b, h: (0, 0))
    return pl.pallas_call(
        kern,
        grid=(bsz, DA_HEADS),
        in_specs=[
            pl.BlockSpec((1, seq, LANES), lambda b, h: (b, 0, h)),
            pl.BlockSpec((1, seq, LANES), lambda b, h: (b, 0, DA_HEADS + h)),
            pl.BlockSpec((1, seq, LANES), lambda b, h: (b, 0, 2 * DA_HEADS + h)),
            vec(DA_HEAD_DIM), vec(DA_HEAD_DIM), vec(DA_HEAD_DIM), vec(DA_HEAD_DIM),
            vec(DA_V_DIM),
        ],
        out_specs=pl.BlockSpec((1, seq, LANES), lambda b, h: (b, 0, h)),
        out_shape=jax.ShapeDtypeStruct((bsz, seq, DA_HEADS * DA_V_DIM), BF16),
        scratch_shapes=[
            pltpu.VMEM((2 * tq, LANES), BF16),
            pltpu.VMEM((2 * tq, tk), F32),
            pltpu.VMEM((2 * tq, tk), F32),
            pltpu.VMEM((2 * tq, LANES), F32),
            pltpu.VMEM((2 * tq, LANES), F32),
            pltpu.VMEM((2 * tq, LANES), F32),
            pltpu.VMEM((2 * tq, LANES), F32),
            pltpu.VMEM((2 * tq, DA_V_DIM), F32),
        ],
        compiler_params=pltpu.CompilerParams(
            dimension_semantics=("arbitrary", "arbitrary"),
            vmem_limit_bytes=VMEM_LIMIT),
        name="diff_attn",
    )(z3, z3, z3, lq1, lk1, lq2, lk2, subln_g)


def _mem_kv_kernel(m_ref, w_ref, o_ref):
    o_ref[...] = _dot(m_ref[...].astype(BF16), w_ref[...]).astype(BF16)


def _mem_kv(mem2, w_b):
    rows, width = mem2.shape[0], w_b.shape[1]
    tn = 1024
    return pl.pallas_call(
        _mem_kv_kernel,
        grid=(width // tn,),
        in_specs=[pl.BlockSpec((rows, D_MODEL), lambda j: (0, 0)),
                  pl.BlockSpec((D_MODEL, tn), lambda j: (0, j))],
        out_specs=pl.BlockSpec((rows, tn), lambda j: (0, j)),
        out_shape=jax.ShapeDtypeStruct((rows, width), BF16),
        compiler_params=pltpu.CompilerParams(
            dimension_semantics=("arbitrary",), vmem_limit_bytes=VMEM_LIMIT),
        name="mem_kv",
    )(mem2, w_b)


def _mix_kernel(u_ref, v_ref, xq_ref, gate_ref, oda_ref, x_ref, mkv_ref, ws_ref, bs_ref,
                wa_ref, wsg_ref, wm_ref, wo_ref, g1_ref, b1_ref, o_ref, osg_ref, oxa_ref, *, tm):
    halves = [slice(h * MIX_HALF, (h + 1) * MIX_HALF) for h in range(tm // MIX_HALF)]
    d = D_MODEL
    row = lax.broadcasted_iota(jnp.int32, (SG_CHUNK, SG_CHUNK), 0)
    col = lax.broadcasted_iota(jnp.int32, (SG_CHUNK, SG_CHUNK), 1)
    ws = [jnp.where(col <= row, ws_ref[g], 0.0).astype(BF16) for g in range(SG_GROUPS)]

    def chunks(hs):
        return [slice(r, r + SG_CHUNK) for r in range(hs.start, hs.stop, SG_CHUNK)]

    def xa_cols(hd):
        return slice(hd * XA_HEAD_DIM, (hd + 1) * XA_HEAD_DIM)

    sg = {}
    for hs in halves:
        for g in range(SG_GROUPS):
            cs = slice(g * SG_GROUP_DIM, (g + 1) * SG_GROUP_DIM)
            for rs in chunks(hs):
                sg[rs.start, g] = _dot(ws[g], v_ref[rs, cs]) + bs_ref[:, g:g + 1]
    xs = [[_dot_nt(xq_ref[hs, xa_cols(hd)], mkv_ref[0, :, xa_cols(hd)]) * (XA_HEAD_DIM ** -0.5)
           for hd in range(XA_HEADS)] for hs in halves]
    t_attn = [_dot(oda_ref[hs, :], wa_ref[...]) for hs in halves]
    for hs in halves:
        for g in range(SG_GROUPS):
            cs = slice(g * SG_GROUP_DIM, (g + 1) * SG_GROUP_DIM)
            for rs in chunks(hs):
                osg_ref[rs, cs] = (u_ref[rs, cs].astype(F32) * sg[rs.start, g]).astype(BF16)
    t_sg = [_dot(osg_ref[hs, :], wsg_ref[...]) for hs in halves]
    for hi, hs in enumerate(halves):
        for hd in range(XA_HEADS):
            s = xs[hi][hd]
            p = jnp.exp(s - jnp.max(s, axis=-1, keepdims=True))
            p = p / jnp.sum(p, axis=-1, keepdims=True)
            vs = slice(XA_HEADS * XA_HEAD_DIM + hd * XA_HEAD_DIM,
                       XA_HEADS * XA_HEAD_DIM + (hd + 1) * XA_HEAD_DIM)
            oxa_ref[hs, xa_cols(hd)] = _dot(p.astype(BF16), mkv_ref[0, :, vs]).astype(BF16)
    t_mem = [_dot(oxa_ref[hs, :], wm_ref[...]) for hs in halves]
    merged = [(gate_ref[hs, 0:d].astype(F32) * t_attn[hi]
               + gate_ref[hs, d:2 * d].astype(F32) * t_sg[hi]
               + gate_ref[hs, 2 * d:3 * d].astype(F32) * t_mem[hi]).astype(BF16)
              for hi, hs in enumerate(halves)]
    y = [_dot(m, wo_ref[...]) for m in merged]
    for hi, hs in enumerate(halves):
        o_ref[hs, :] = _layer_norm(ALPHA * x_ref[hs, :] + y[hi], g1_ref[...], b1_ref[...])


def _mix(z2, oda2, x2, mkv3, w_s, b_s_t, wa, wsg, wm, wo, ln_g, ln_b, tm, seq):
    n = x2.shape[0]
    d = D_MODEL
    full = lambda shape: pl.BlockSpec(shape, lambda i: (0,) * len(shape))
    return pl.pallas_call(
        functools.partial(_mix_kernel, tm=tm),
        grid=(n // tm,),
        in_specs=[
            pl.BlockSpec((tm, d), lambda i: (i, 3)),
            pl.BlockSpec((tm, d), lambda i: (i, 4)),
            pl.BlockSpec((tm, d), lambda i: (i, 5)),
            pl.BlockSpec((tm, 3 * d), lambda i: (i, 2)),
            pl.BlockSpec((tm, d), lambda i: (i, 0)),
            pl.BlockSpec((tm, d), lambda i: (i, 0)),
            pl.BlockSpec((1, mkv3.shape[1], mkv3.shape[2]), lambda i: ((i * tm) // seq, 0, 0)),
            full(w_s.shape), full(b_s_t.shape),
            full((d, d)), full((d, d)), full((d, d)), full((d, d)),
            full((1, d)), full((1, d)),
        ],
        out_specs=pl.BlockSpec((tm, d), lambda i: (i, 0)),
        out_shape=jax.ShapeDtypeStruct((n, d), F32),
        scratch_shapes=[pltpu.VMEM((tm, d), BF16), pltpu.VMEM((tm, d), BF16)],
        compiler_params=pltpu.CompilerParams(
            dimension_semantics=("arbitrary",), vmem_limit_bytes=VMEM_LIMIT),
        name="mix",
    )(z2, z2, z2, z2, oda2, x2, mkv3, w_s, b_s_t, wa, wsg, wm, wo, ln_g, ln_b)


def _ffn_kernel(x_ref, wi_ref, wo_ref, g_ref, b_ref, o_ref, *, d_ff, tm):
    halves = [slice(h * MIX_HALF, (h + 1) * MIX_HALF) for h in range(tm // MIX_HALF)]
    hid = [_dot(x_ref[hs, :].astype(BF16), wi_ref[...]) for hs in halves]
    act = [((h[:, :d_ff] * _sigmoid(h[:, :d_ff])) * h[:, d_ff:]).astype(BF16) for h in hid]
    y = [_dot(a, wo_ref[...]) for a in act]
    for hi, hs in enumerate(halves):
        o_ref[hs, :] = _layer_norm(ALPHA * x_ref[hs, :] + y[hi], g_ref[...], b_ref[...])


def _ffn(x1, wi, wo, ln_g, ln_b, tm):
    n, d = x1.shape
    d_ff = wo.shape[0]
    full = lambda shape: pl.BlockSpec(shape, lambda i: (0,) * len(shape),
                                      pipeline_mode=pl.Buffered(1))
    return pl.pallas_call(
        functools.partial(_ffn_kernel, d_ff=d_ff, tm=tm),
        grid=(n // tm,),
        in_specs=[pl.BlockSpec((tm, d), lambda i: (i, 0)),
                  full(wi.shape), full(wo.shape), full((1, d)), full((1, d))],
        out_specs=pl.BlockSpec((tm, d), lambda i: (i, 0)),
        out_shape=jax.ShapeDtypeStruct((n, d), F32),
        compiler_params=pltpu.CompilerParams(
            dimension_semantics=("arbitrary",), vmem_limit_bytes=VMEM_LIMIT),
        name="ffn",
    )(x1, wi, wo, ln_g, ln_b)


def kernel(x, mem, positions, w_in, lambda_q1, lambda_k1, lambda_q2, lambda_k2, da_subln_g,
           sg_norm_g, sg_norm_b, sg_w_s, sg_b_s, w_mem_kv, w_br_attn, w_br_sg, w_br_mem, w_out,
           ln1_g, ln1_b, w_ffn_in, w_ffn_out, ln2_g, ln2_b):
    bsz, seq, d = x.shape
    n = bsz * seq
    depth = w_in.shape[0]
    assert d == D_MODEL and depth == DEPTH

    inv_freq = ROPE_THETA ** (-jnp.arange(ROT_HALF, dtype=F32) * 2.0 / ROT_DIM)
    invf = inv_freq.reshape(ROT_HALF, 1)
    pos2 = positions.reshape(n // PROJ_ROWS, 1, PROJ_ROWS)

    x2 = x.reshape(n, d)
    for l in range(depth):
        lambda_init = 0.8 - 0.6 * math.exp(-0.3 * l)
        row = lambda a: a[l].reshape(1, -1).astype(F32)
        z2 = _proj_in(x2, w_in[l].astype(BF16), pos2, invf, row(sg_norm_g), row(sg_norm_b),
                      tm=PROJ_ROWS)
        oda = _diff_attn(z2.reshape(bsz, seq, -1), row(lambda_q1), row(lambda_k1),
                         row(lambda_q2), row(lambda_k2), row(da_subln_g), tq=ATTN_Q_ROWS, tk=ATTN_K_ROWS,
                         lambda_init=lambda_init)
        mkv = _mem_kv(mem.reshape(-1, d), w_mem_kv[l].astype(BF16)).reshape(bsz, mem.shape[1], -1)
        x1 = _mix(z2, oda.reshape(n, -1), x2, mkv, sg_w_s[l], sg_b_s[l].T,
                  w_br_attn[l].astype(BF16), w_br_sg[l].astype(BF16), w_br_mem[l].astype(BF16),
                  w_out[l].astype(BF16), row(ln1_g), row(ln1_b), tm=MIX_ROWS, seq=seq)
        x2 = _ffn(x1, w_ffn_in[l].astype(BF16), w_ffn_out[l].astype(BF16), row(ln2_g), row(ln2_b),
                  tm=FFN_ROWS)
    return x2.reshape(bsz, seq, d)
```

```python
import functools
import math

import jax
import jax.numpy as jnp
from jax import lax
from jax.experimental import pallas as pl
from jax.experimental.pallas import tpu as pltpu

D_MODEL = 1024
DA_HEADS = 8
DA_HEAD_DIM = 64
DA_V_DIM = 2 * DA_HEAD_DIM
ROPE_THETA = 500000.0
ROT_DIM = DA_HEAD_DIM // 4
ROT_HALF = ROT_DIM // 2
SG_GROUPS = 8
SG_CHUNK = 128
SG_GROUP_DIM = 128
XA_HEADS = 4
XA_HEAD_DIM = 256
N_SECTIONS = 9
SECTIONS_PER_GROUP = 3
DEPTH = 1
ALPHA = (2 * DEPTH) ** 0.25
LN_EPS = 1e-5
RMS_EPS = 1e-5
LANES = 128
COL_CHUNK = 256
MIX_HALF = 256

V7X_VMEM_BYTES = 64 * 1024 * 1024
VMEM_LIMIT = V7X_VMEM_BYTES * 7 // 8

PROJ_ROWS = 1024
ATTN_Q_ROWS = 512
ATTN_K_ROWS = 1024
MIX_ROWS = 512
FFN_ROWS = 512

F32 = jnp.float32
BF16 = jnp.bfloat16


def _dot(a, b):
    return jnp.dot(a, b, preferred_element_type=F32)


def _dot_nt(a, b):
    return lax.dot_general(a, b, (((1,), (1,)), ((), ())), preferred_element_type=F32)


def _sigmoid(x):
    return 0.5 + 0.5 * jnp.tanh(0.5 * x)


def _gelu_tanh(x):
    c = math.sqrt(2.0 / math.pi)
    return x * (0.5 + 0.5 * jnp.tanh(x * (c + (0.044715 * c) * (x * x))))


def _layer_norm(x, g, b):
    mu = jnp.mean(x, axis=-1, keepdims=True)
    xc = x - mu
    var = jnp.mean(xc * xc, axis=-1, keepdims=True)
    return xc * lax.rsqrt(var + LN_EPS) * g + b


def _proj_in_kernel(x_ref, w_ref, pos_ref, invf_ref, sgg_ref, sgb_ref, o_ref,
                    xb_ref, cos_ref, sin_ref, gel_ref, acc_ref):
    grp = pl.program_id(1)
    nchunk = D_MODEL // COL_CHUNK
    pending = []
    issued = [0]

    def flush():
        while pending:
            pending.pop(0)()

    def chunk(sec, c, epilogue):
        cs = slice(sec * D_MODEL + c * COL_CHUNK, sec * D_MODEL + (c + 1) * COL_CHUNK)
        slot = issued[0] % nchunk
        issued[0] += 1
        park = slice(slot * COL_CHUNK, (slot + 1) * COL_CHUNK)
        acc_ref[:, park] = _dot(xb_ref[...], w_ref[:, cs])
        flush()
        pending.append(lambda: epilogue(acc_ref[:, park], cs))

    def run(sec, epilogue):
        for c in range(nchunk):
            chunk(sec, c, epilogue)

    def plain(acc, cs):
        o_ref[:, cs] = acc.astype(BF16)

    def rope(scale):
        d = lax.broadcasted_iota(jnp.int32, (1, LANES), 1) % DA_HEAD_DIM
        sgn_lo = jnp.where(d < ROT_HALF, -scale, 0.0)
        sgn_hi = jnp.where((d >= ROT_HALF) & (d < ROT_DIM), scale, 0.0)

        def epilogue(acc, cs):
            cos = cos_ref[...] * scale
            sin_lo = sin_ref[...] * sgn_lo
            sin_hi = sin_ref[...] * sgn_hi
            for h in range(COL_CHUNK // LANES):
                t = acc[:, h * LANES:(h + 1) * LANES]
                t_up = pltpu.roll(t, LANES - ROT_HALF, 1)
                t_dn = pltpu.roll(t, ROT_HALF, 1)
                r = t * cos + t_up * sin_lo + t_dn * sin_hi
                o_ref[:, cs.start + h * LANES:cs.start + (h + 1) * LANES] = r.astype(BF16)
        return epilogue

    @pl.when(grp == 0)
    def _():
        xb_ref[...] = x_ref[...].astype(BF16)
        ang = invf_ref[...] * pos_ref[0].astype(F32)
        reps = LANES // ROT_HALF
        lane = lax.broadcasted_iota(jnp.int32, (1, LANES), 1)
        rotary = (lane % DA_HEAD_DIM) < ROT_DIM
        cos_ref[...] = jnp.where(rotary, jnp.tile(jnp.cos(ang), (reps, 1)).T, 1.0)
        sin_ref[...] = jnp.where(rotary, jnp.tile(jnp.sin(ang), (reps, 1)).T, 0.0)
        run(0, rope(DA_HEAD_DIM ** -0.5 * math.log2(math.e)))
        run(1, rope(1.0))
        run(2, plain)
        flush()

    @pl.when(grp == 1)
    def _():
        def gelu_keep(acc, cs):
            gel_ref[:, cs.start - D_MODEL:cs.stop - D_MODEL] = _gelu_tanh(acc)

        def gelu_out(acc, cs):
            o_ref[:, cs] = _gelu_tanh(acc).astype(BF16)

        def norm_sv():
            o_ref[:, D_MODEL:2 * D_MODEL] = _layer_norm(
                gel_ref[...], sgg_ref[...], sgb_ref[...]).astype(BF16)

        run(1, gelu_keep)
        pending.append(norm_sv)
        run(0, gelu_out)
        run(2, plain)
        flush()

    @pl.when(grp == 2)
    def _():
        def gate(acc, cs):
            o_ref[:, cs] = _sigmoid(acc).astype(BF16)
        for sec in range(SECTIONS_PER_GROUP):
            run(sec, gate)
        flush()


def _proj_in(x2, w_in_b, pos2, invf, sgg, sgb, tm):
    n = x2.shape[0]
    gw = SECTIONS_PER_GROUP * D_MODEL
    return pl.pallas_call(
        _proj_in_kernel,
        grid=(n // tm, N_SECTIONS // SECTIONS_PER_GROUP),
        in_specs=[
            pl.BlockSpec((tm, D_MODEL), lambda i, j: (i, 0)),
            pl.BlockSpec((D_MODEL, gw), lambda i, j: (0, j)),
            pl.BlockSpec((1, 1, tm), lambda i, j: (i, 0, 0)),
            pl.BlockSpec((ROT_HALF, 1), lambda i, j: (0, 0)),
            pl.BlockSpec((1, D_MODEL), lambda i, j: (0, 0)),
            pl.BlockSpec((1, D_MODEL), lambda i, j: (0, 0)),
        ],
        out_specs=pl.BlockSpec((tm, gw), lambda i, j: (i, j)),
        out_shape=jax.ShapeDtypeStruct((n, N_SECTIONS * D_MODEL), BF16),
        scratch_shapes=[
            pltpu.VMEM((tm, D_MODEL), BF16),
            pltpu.VMEM((tm, LANES), F32),
            pltpu.VMEM((tm, LANES), F32),
            pltpu.VMEM((tm, D_MODEL), F32),
            pltpu.VMEM((tm, D_MODEL), F32),
        ],
        compiler_params=pltpu.CompilerParams(
            dimension_semantics=("arbitrary", "arbitrary"), vmem_limit_bytes=VMEM_LIMIT),
        name="proj_in",
    )(x2, w_in_b, pos2, invf, sgg, sgb)


def _diff_attn_kernel(q_ref, k_ref, v_ref, lq1_ref, lk1_ref, lq2_ref, lk2_ref, g_ref, o_ref,
                      qs_ref, s0_ref, s1_ref, mx0_ref, mx1_ref, m_ref, l_ref, acc_ref,
                      *, tq, tk, nq, lambda_init):
    s_ref = (s0_ref, s1_ref)
    mx_ref = (mx0_ref, mx1_ref)
    rows = 2 * tq
    lane = lax.broadcasted_iota(jnp.int32, (1, LANES), 1)

    def look_ahead(jb, slot, width):
        start = pl.multiple_of(jb * tk, tk)
        kb = k_ref[0, pl.ds(start, width), :]
        s = _dot_nt(qs_ref[...], kb)
        smax = s[:, 0:LANES]
        for g in range(1, width // LANES):
            smax = jnp.maximum(smax, s[:, g * LANES:(g + 1) * LANES])
        s_ref[slot][:, 0:width] = s
        mx_ref[slot][...] = smax

    def begin(qi, width):
        q = q_ref[0, pl.ds(pl.multiple_of(qi * tq, tq), tq), :]
        qs_ref[0:tq, :] = jnp.where(lane < DA_HEAD_DIM, q, jnp.zeros_like(q))
        qs_ref[tq:rows, :] = jnp.where(lane >= DA_HEAD_DIM, q, jnp.zeros_like(q))
        look_ahead(0, 0, width)
        m_ref[...] = jnp.full(m_ref.shape, -jnp.inf, F32)
        l_ref[...] = jnp.zeros(l_ref.shape, F32)
        acc_ref[...] = jnp.zeros(acc_ref.shape, F32)

    def finish(qi):
        lam = (jnp.exp(jnp.sum(lq1_ref[...] * lk1_ref[...], axis=-1, keepdims=True))
               - jnp.exp(jnp.sum(lq2_ref[...] * lk2_ref[...], axis=-1, keepdims=True))
               + lambda_init)
        o = (acc_ref[0:tq, :] / l_ref[0:tq, :]
             - lam * (acc_ref[tq:rows, :] / l_ref[tq:rows, :]))
        ms = jnp.mean(o * o, axis=-1, keepdims=True)
        o = o * lax.rsqrt(ms + RMS_EPS) * g_ref[...] * (1.0 - lambda_init)
        o_ref[0, pl.ds(pl.multiple_of(qi * tq, tq), tq), :] = o.astype(BF16)

    def softmax_pv(jb, slot, width, mask_from, smax=None):
        ngrp = width // LANES
        start = pl.multiple_of(jb * tk, tk)
        vb = v_ref[0, pl.ds(start, width), :]
        row = lax.broadcasted_iota(jnp.int32, (rows, LANES), 0) & (tq - 1)
        col = lax.broadcasted_iota(jnp.int32, (rows, LANES), 1)

        def sgrp(g):
            x = s_ref[slot][:, g * LANES:(g + 1) * LANES]
            if g >= mask_from:
                x = jnp.where(col + (g - mask_from) * LANES <= row, x, -jnp.inf)
            return x

        if smax is None:
            smax = sgrp(0)
            for g in range(1, ngrp):
                smax = jnp.maximum(smax, sgrp(g))
        m_old = m_ref[...]
        m_new = jnp.maximum(m_old, jnp.max(smax, axis=-1, keepdims=True))
        m_ref[...] = m_new
        a = jnp.exp2(m_old - m_new)
        p = jnp.concatenate([jnp.exp2((sgrp(g) - m_new).astype(BF16)) for g in range(ngrp)],
                            axis=1)
        v1 = jnp.concatenate([vb, jnp.ones((width, LANES), BF16)], axis=1)
        pv = _dot(p, v1)
        acc_ref[...] = a * acc_ref[...] + pv[:, :DA_V_DIM]
        l_ref[...] = a * l_ref[...] + pv[:, DA_V_DIM:]

    def step(jb, slot, next_width=tk):
        look_ahead(jb + 1, 1 - slot, next_width)
        softmax_pv(jb, slot, tk, tk // LANES, smax=mx_ref[slot][...])

    def pair(p, carry):
        step(2 * p, 0)
        step(2 * p + 1, 1)
        return carry

    def query_block(i, carry):
        nfull = lax.shift_right_logical(i, 1)
        npair = lax.shift_right_logical(nfull, 1)
        even_i = (i & 1) == 0
        peel = ((nfull & 1) == 0) & (npair >= 1)
        lax.fori_loop(0, npair - peel.astype(jnp.int32), pair, 0)

        def last_pair(width):
            step(nfull - 2, 0)
            step(nfull - 1, 1, width)

        pl.when(peel & even_i)(functools.partial(last_pair, tq))
        pl.when(peel & jnp.logical_not(even_i))(functools.partial(last_pair, tk))

        def last_blocks(slot, width, mask_from):
            if slot == 1:
                step(nfull - 1, 0, width)
            softmax_pv(nfull, slot, width, mask_from)

        for slot in range(2):
            pl.when(even_i & ((nfull & 1) == slot))(functools.partial(last_blocks, slot, tq, 0))
            pl.when(jnp.logical_not(even_i) & ((nfull & 1) == slot))(
                functools.partial(last_blocks, slot, tk, tq // LANES))

        @pl.when(i + 1 < nq)
        def _():
            finish(i)
            begin(i + 1, tk)

        @pl.when(i + 1 == nq)
        def _():
            finish(i)

        return carry

    begin(0, tq)
    lax.fori_loop(0, nq, query_block, 0)


def _diff_attn(z3, lq1, lk1, lq2, lk2, subln_g, tq, tk, lambda_init):
    bsz, seq, _ = z3.shape
    assert tk == 2 * tq
    kern = functools.partial(_diff_attn_kernel, tq=tq, tk=tk, nq=seq // tq,
                             lambda_init=lambda_init)
    vec = lambda w: pl.BlockSpec((1, w), lambda b, h: (0, 0))
    return pl.pallas_call(
        kern,
        grid=(bsz, DA_HEADS),
        in_specs=[
            pl.BlockSpec((1, seq, LANES), lambda b, h: (b, 0, h)),
            pl.BlockSpec((1, seq, LANES), lambda b, h: (b, 0, DA_HEADS + h)),
            pl.BlockSpec((1, seq, LANES), lambda b, h: (b, 0, 2 * DA_HEADS + h)),
            vec(DA_HEAD_DIM), vec(DA_HEAD_DIM), vec(DA_HEAD_DIM), vec(DA_HEAD_DIM),
            vec(DA_V_DIM),
        ],
        out_specs=pl.BlockSpec((1, seq, LANES), lambda b, h: (b, 0, h)),
        out_shape=jax.ShapeDtypeStruct((bsz, seq, DA_HEADS * DA_V_DIM), BF16),
        scratch_shapes=[
            pltpu.VMEM((2 * tq, LANES), BF16),
            pltpu.VMEM((2 * tq, tk), F32),
            pltpu.VMEM((2 * tq, tk), F32),
            pltpu.VMEM((2 * tq, LANES), F32),
            pltpu.VMEM((2 * tq, LANES), F32),
            pltpu.VMEM((2 * tq, LANES), F32),
            pltpu.VMEM((2 * tq, LANES), F32),
            pltpu.VMEM((2 * tq, DA_V_DIM), F32),
        ],
        compiler_params=pltpu.CompilerParams(
            dimension_semantics=("arbitrary", "arbitrary"),
            vmem_limit_bytes=VMEM_LIMIT),
        name="diff_attn",
    )(z3, z3, z3, lq1, lk1, lq2, lk2, subln_g)


def _mem_kv_kernel(m_ref, w_ref, o_ref):
    o_ref[...] = _dot(m_ref[...].astype(BF16), w_ref[...]).astype(BF16)


def _mem_kv(mem2, w_b):
    rows, width = mem2.shape[0], w_b.shape[1]
    tn = 1024
    return pl.pallas_call(
        _mem_kv_kernel,
        grid=(width // tn,),
        in_specs=[pl.BlockSpec((rows, D_MODEL), lambda j: (0, 0)),
                  pl.BlockSpec((D_MODEL, tn), lambda j: (0, j))],
        out_specs=pl.BlockSpec((rows, tn), lambda j: (0, j)),
        out_shape=jax.ShapeDtypeStruct((rows, width), BF16),
        compiler_params=pltpu.CompilerParams(
            dimension_semantics=("arbitrary",), vmem_limit_bytes=VMEM_LIMIT),
        name="mem_kv",
    )(mem2, w_b)


def _mix_kernel(u_ref, v_ref, xq_ref, gate_ref, oda_ref, x_ref, mkv_ref, ws_ref, bs_ref,
                wa_ref, wsg_ref, wm_ref, wo_ref, g1_ref, b1_ref, o_ref, osg_ref, oxa_ref, *, tm):
    halves = [slice(h * MIX_HALF, (h + 1) * MIX_HALF) for h in range(tm // MIX_HALF)]
    d = D_MODEL
    row = lax.broadcasted_iota(jnp.int32, (SG_CHUNK, SG_CHUNK), 0)
    col = lax.broadcasted_iota(jnp.int32, (SG_CHUNK, SG_CHUNK), 1)
    ws = [jnp.where(col <= row, ws_ref[g], 0.0).astype(BF16) for g in range(SG_GROUPS)]

    def chunks(hs):
        return [slice(r, r + SG_CHUNK) for r in range(hs.start, hs.stop, SG_CHUNK)]

    def xa_cols(hd):
        return slice(hd * XA_HEAD_DIM, (hd + 1) * XA_HEAD_DIM)

    sg = {}
    for hs in halves:
        for g in range(SG_GROUPS):
            cs = slice(g * SG_GROUP_DIM, (g + 1) * SG_GROUP_DIM)
            for rs in chunks(hs):
                sg[rs.start, g] = _dot(ws[g], v_ref[rs, cs]) + bs_ref[:, g:g + 1]
    xs = [[_dot_nt(xq_ref[hs, xa_cols(hd)], mkv_ref[0, :, xa_cols(hd)]) * (XA_HEAD_DIM ** -0.5)
           for hd in range(XA_HEADS)] for hs in halves]
    t_attn = [_dot(oda_ref[hs, :], wa_ref[...]) for hs in halves]
    for hs in halves:
        for g in range(SG_GROUPS):
            cs = slice(g * SG_GROUP_DIM, (g + 1) * SG_GROUP_DIM)
            for rs in chunks(hs):
                osg_ref[rs, cs] = (u_ref[rs, cs].astype(F32) * sg[rs.start, g]).astype(BF16)
    t_sg = [_dot(osg_ref[hs, :], wsg_ref[...]) for hs in halves]
    for hi, hs in enumerate(halves):
        for hd in range(XA_HEADS):
            s = xs[hi][hd]
            p = jnp.exp(s - jnp.max(s, axis=-1, keepdims=True))
            p = p / jnp.sum(p, axis=-1, keepdims=True)
            vs = slice(XA_HEADS * XA_HEAD_DIM + hd * XA_HEAD_DIM,
                       XA_HEADS * XA_HEAD_DIM + (hd + 1) * XA_HEAD_DIM)
            oxa_ref[hs, xa_cols(hd)] = _dot(p.astype(BF16), mkv_ref[0, :, vs]).astype(BF16)
    t_mem = [_dot(oxa_ref[hs, :], wm_ref[...]) for hs in halves]
    merged = [(gate_ref[hs, 0:d].astype(F32) * t_attn[hi]
               + gate_ref[hs, d:2 * d].astype(F32) * t_sg[hi]
               + gate_ref[hs, 2 * d:3 * d].astype(F32) * t_mem[hi]).astype(BF16)
              for hi, hs in enumerate(halves)]
    y = [_dot(m, wo_ref[...]) for m in merged]
    for hi, hs in enumerate(halves):
        o_ref[hs, :] = _layer_norm(ALPHA * x_ref[hs, :] + y[hi], g1_ref[...], b1_ref[...])


def _mix(z2, oda2, x2, mkv3, w_s, b_s_t, wa, wsg, wm, wo, ln_g, ln_b, tm, seq):
    n = x2.shape[0]
    d = D_MODEL
    full = lambda shape: pl.BlockSpec(shape, lambda i: (0,) * len(shape))
    return pl.pallas_call(
        functools.partial(_mix_kernel, tm=tm),
        grid=(n // tm,),
        in_specs=[
            pl.BlockSpec((tm, d), lambda i: (i, 3)),
            pl.BlockSpec((tm, d), lambda i: (i, 4)),
            pl.BlockSpec((tm, d), lambda i: (i, 5)),
            pl.BlockSpec((tm, 3 * d), lambda i: (i, 2)),
            pl.BlockSpec((tm, d), lambda i: (i, 0)),
            pl.BlockSpec((tm, d), lambda i: (i, 0)),
            pl.BlockSpec((1, mkv3.shape[1], mkv3.shape[2]), lambda i: ((i * tm) // seq, 0, 0)),
            full(w_s.shape), full(b_s_t.shape),
            full((d, d)), full((d, d)), full((d, d)), full((d, d)),
            full((1, d)), full((1, d)),
        ],
        out_specs=pl.BlockSpec((tm, d), lambda i: (i, 0)),
        out_shape=jax.ShapeDtypeStruct((n, d), F32),
        scratch_shapes=[pltpu.VMEM((tm, d), BF16), pltpu.VMEM((tm, d), BF16)],
        compiler_params=pltpu.CompilerParams(
            dimension_semantics=("arbitrary",), vmem_limit_bytes=VMEM_LIMIT),
        name="mix",
    )(z2, z2, z2, z2, oda2, x2, mkv3, w_s, b_s_t, wa, wsg, wm, wo, ln_g, ln_b)


def _ffn_kernel(x_ref, wi_ref, wo_ref, g_ref, b_ref, o_ref, *, d_ff, tm):
    halves = [slice(h * MIX_HALF, (h + 1) * MIX_HALF) for h in range(tm // MIX_HALF)]
    hid = [_dot(x_ref[hs, :].astype(BF16), wi_ref[...]) for hs in halves]
    act = [((h[:, :d_ff] * _sigmoid(h[:, :d_ff])) * h[:, d_ff:]).astype(BF16) for h in hid]
    y = [_dot(a, wo_ref[...]) for a in act]
    for hi, hs in enumerate(halves):
        o_ref[hs, :] = _layer_norm(ALPHA * x_ref[hs, :] + y[hi], g_ref[...], b_ref[...])


def _ffn(x1, wi, wo, ln_g, ln_b, tm):
    n, d = x1.shape
    d_ff = wo.shape[0]
    full = lambda shape: pl.BlockSpec(shape, lambda i: (0,) * len(shape),
                                      pipeline_mode=pl.Buffered(1))
    return pl.pallas_call(
        functools.partial(_ffn_kernel, d_ff=d_ff, tm=tm),
        grid=(n // tm,),
        in_specs=[pl.BlockSpec((tm, d), lambda i: (i, 0)),
                  full(wi.shape), full(wo.shape), full((1, d)), full((1, d))],
        out_specs=pl.BlockSpec((tm, d), lambda i: (i, 0)),
        out_shape=jax.ShapeDtypeStruct((n, d), F32),
        compiler_params=pltpu.CompilerParams(
            dimension_semantics=("arbitrary",), vmem_limit_bytes=VMEM_LIMIT),
        name="ffn",
    )(x1, wi, wo, ln_g, ln_b)


def kernel(x, mem, positions, w_in, lambda_q1, lambda_k1, lambda_q2, lambda_k2, da_subln_g,
           sg_norm_g, sg_norm_b, sg_w_s, sg_b_s, w_mem_kv, w_br_attn, w_br_sg, w_br_mem, w_out,
           ln1_g, ln1_b, w_ffn_in, w_ffn_out, ln2_g, ln2_b):
    bsz, seq, d = x.shape
    n = bsz * seq
    depth = w_in.shape[0]
    assert d == D_MODEL and depth == DEPTH

    inv_freq = ROPE_THETA ** (-jnp.arange(ROT_HALF, dtype=F32) * 2.0 / ROT_DIM)
    invf = inv_freq.reshape(ROT_HALF, 1)
    pos2 = positions.reshape(n // PROJ_ROWS, 1, PROJ_ROWS)

    x2 = x.reshape(n, d)
    for l in range(depth):
        lambda_init = 0.8 - 0.6 * math.exp(-0.3 * l)
        row = lambda a: a[l].reshape(1, -1).astype(F32)
        z2 = _proj_in(x2, w_in[l].astype(BF16), pos2, invf, row(sg_norm_g), row(sg_norm_b),
                      tm=PROJ_ROWS)
        oda = _diff_attn(z2.reshape(bsz, seq, -1), row(lambda_q1), row(lambda_k1),
                         row(lambda_q2), row(lambda_k2), row(da_subln_g), tq=ATTN_Q_ROWS, tk=ATTN_K_ROWS,
                         lambda_init=lambda_init)
        mkv = _mem_kv(mem.reshape(-1, d), w_mem_kv[l].astype(BF16)).reshape(bsz, mem.shape[1], -1)
        x1 = _mix(z2, oda.reshape(n, -1), x2, mkv, sg_w_s[l], sg_b_s[l].T,
                  w_br_attn[l].astype(BF16), w_br_sg[l].astype(BF16), w_br_mem[l].astype(BF16),
                  w_out[l].astype(BF16), row(ln1_g), row(ln1_b), tm=MIX_ROWS, seq=seq)
        x2 = _ffn(x1, w_ffn_in[l].astype(BF16), w_ffn_out[l].astype(BF16), row(ln2_g), row(ln2_b),
                  tm=FFN_ROWS)
    return x2.reshape(bsz, seq, d)
```

```python
import functools
import math

import jax
import jax.numpy as jnp
from jax import lax
from jax.experimental import pallas as pl
from jax.experimental.pallas import tpu as pltpu

D_MODEL = 1024
DA_HEADS = 8
DA_HEAD_DIM = 64
DA_V_DIM = 2 * DA_HEAD_DIM
ROPE_THETA = 500000.0
ROT_DIM = DA_HEAD_DIM // 4
ROT_HALF = ROT_DIM // 2
SG_GROUPS = 8
SG_CHUNK = 128
SG_GROUP_DIM = 128
XA_HEADS = 4
XA_HEAD_DIM = 256
N_SECTIONS = 9
SECTIONS_PER_GROUP = 3
DEPTH = 1
ALPHA = (2 * DEPTH) ** 0.25
LN_EPS = 1e-5
RMS_EPS = 1e-5
LANES = 128
COL_CHUNK = 256
MIX_HALF = 256

V7X_VMEM_BYTES = 64 * 1024 * 1024
VMEM_LIMIT = V7X_VMEM_BYTES * 7 // 8

PROJ_ROWS = 1024
ATTN_Q_ROWS = 512
ATTN_K_ROWS = 1024
MIX_ROWS = 512
FFN_ROWS = 512

F32 = jnp.float32
BF16 = jnp.bfloat16


def _dot(a, b):
    return jnp.dot(a, b, preferred_element_type=F32)


def _dot_nt(a, b):
    return lax.dot_general(a, b, (((1,), (1,)), ((), ())), preferred_element_type=F32)


def _sigmoid(x):
    return 0.5 + 0.5 * jnp.tanh(0.5 * x)


def _gelu_tanh(x):
    c = math.sqrt(2.0 / math.pi)
    return x * (0.5 + 0.5 * jnp.tanh(x * (c + (0.044715 * c) * (x * x))))


def _layer_norm(x, g, b):
    mu = jnp.mean(x, axis=-1, keepdims=True)
    xc = x - mu
    var = jnp.mean(xc * xc, axis=-1, keepdims=True)
    return xc * lax.rsqrt(var + LN_EPS) * g + b


def _proj_in_kernel(x_ref, w_ref, pos_ref, invf_ref, sgg_ref, sgb_ref, o_ref,
                    xb_ref, cos_ref, sin_ref, gel_ref, acc_ref):
    grp = pl.program_id(1)
    nchunk = D_MODEL // COL_CHUNK
    pending = []
    issued = [0]

    def flush():
        while pending:
            pending.pop(0)()

    def chunk(sec, c, epilogue):
        cs = slice(sec * D_MODEL + c * COL_CHUNK, sec * D_MODEL + (c + 1) * COL_CHUNK)
        slot = issued[0] % nchunk
        issued[0] += 1
        park = slice(slot * COL_CHUNK, (slot + 1) * COL_CHUNK)
        acc_ref[:, park] = _dot(xb_ref[...], w_ref[:, cs])
        flush()
        pending.append(lambda: epilogue(acc_ref[:, park], cs))

    def run(sec, epilogue):
        for c in range(nchunk):
            chunk(sec, c, epilogue)

    def plain(acc, cs):
        o_ref[:, cs] = acc.astype(BF16)

    def rope(scale):
        d = lax.broadcasted_iota(jnp.int32, (1, LANES), 1) % DA_HEAD_DIM
        sgn_lo = jnp.where(d < ROT_HALF, -scale, 0.0)
        sgn_hi = jnp.where((d >= ROT_HALF) & (d < ROT_DIM), scale, 0.0)

        def epilogue(acc, cs):
            cos = cos_ref[...] * scale
            sin_lo = sin_ref[...] * sgn_lo
            sin_hi = sin_ref[...] * sgn_hi
            for h in range(COL_CHUNK // LANES):
                t = acc[:, h * LANES:(h + 1) * LANES]
                t_up = pltpu.roll(t, LANES - ROT_HALF, 1)
                t_dn = pltpu.roll(t, ROT_HALF, 1)
                r = t * cos + t_up * sin_lo + t_dn * sin_hi
                o_ref[:, cs.start + h * LANES:cs.start + (h + 1) * LANES] = r.astype(BF16)
        return epilogue

    @pl.when(grp == 0)
    def _():
        xb_ref[...] = x_ref[...].astype(BF16)
        ang = invf_ref[...] * pos_ref[0].astype(F32)
        reps = LANES // ROT_HALF
        lane = lax.broadcasted_iota(jnp.int32, (1, LANES), 1)
        rotary = (lane % DA_HEAD_DIM) < ROT_DIM
        cos_ref[...] = jnp.where(rotary, jnp.tile(jnp.cos(ang), (reps, 1)).T, 1.0)
        sin_ref[...] = jnp.where(rotary, jnp.tile(jnp.sin(ang), (reps, 1)).T, 0.0)
        run(0, rope(DA_HEAD_DIM ** -0.5 * math.log2(math.e)))
        run(1, rope(1.0))
        run(2, plain)
        flush()

    @pl.when(grp == 1)
    def _():
        def gelu_keep(acc, cs):
            gel_ref[:, cs.start - D_MODEL:cs.stop - D_MODEL] = _gelu_tanh(acc)

        def gelu_out(acc, cs):
            o_ref[:, cs] = _gelu_tanh(acc).astype(BF16)

        def norm_sv():
            o_ref[:, D_MODEL:2 * D_MODEL] = _layer_norm(
                gel_ref[...], sgg_ref[...], sgb_ref[...]).astype(BF16)

        run(1, gelu_keep)
        pending.append(norm_sv)
        run(0, gelu_out)
        run(2, plain)
        flush()

    @pl.when(grp == 2)
    def _():
        def gate(acc, cs):
            o_ref[:, cs] = _sigmoid(acc).astype(BF16)
        for sec in range(SECTIONS_PER_GROUP):
            run(sec, gate)
        flush()


def _proj_in(x2, w_in_b, pos2, invf, sgg, sgb, tm):
    n = x2.shape[0]
    gw = SECTIONS_PER_GROUP * D_MODEL
    return pl.pallas_call(
        _proj_in_kernel,
        grid=(n // tm, N_SECTIONS // SECTIONS_PER_GROUP),
        in_specs=[
            pl.BlockSpec((tm, D_MODEL), lambda i, j: (i, 0)),
            pl.BlockSpec((D_MODEL, gw), lambda i, j: (0, j)),
            pl.BlockSpec((1, 1, tm), lambda i, j: (i, 0, 0)),
            pl.BlockSpec((ROT_HALF, 1), lambda i, j: (0, 0)),
            pl.BlockSpec((1, D_MODEL), lambda i, j: (0, 0)),
            pl.BlockSpec((1, D_MODEL), lambda i, j: (0, 0)),
        ],
        out_specs=pl.BlockSpec((tm, gw), lambda i, j: (i, j)),
        out_shape=jax.ShapeDtypeStruct((n, N_SECTIONS * D_MODEL), BF16),
        scratch_shapes=[
            pltpu.VMEM((tm, D_MODEL), BF16),
            pltpu.VMEM((tm, LANES), F32),
            pltpu.VMEM((tm, LANES), F32),
            pltpu.VMEM((tm, D_MODEL), F32),
            pltpu.VMEM((tm, D_MODEL), F32),
        ],
        compiler_params=pltpu.CompilerParams(
            dimension_semantics=("arbitrary", "arbitrary"), vmem_limit_bytes=VMEM_LIMIT),
        name="proj_in",
    )(x2, w_in_b, pos2, invf, sgg, sgb)


def _diff_attn_kernel(q_ref, k_ref, v_ref, lq1_ref, lk1_ref, lq2_ref, lk2_ref, g_ref, o_ref,
                      qs_ref, s0_ref, s1_ref, mx0_ref, mx1_ref, m_ref, l_ref, acc_ref,
                      *, tq, tk, nq, lambda_init):
    s_ref = (s0_ref, s1_ref)
    mx_ref = (mx0_ref, mx1_ref)
    rows = 2 * tq
    lane = lax.broadcasted_iota(jnp.int32, (1, LANES), 1)

    def look_ahead(jb, slot, width):
        start = pl.multiple_of(jb * tk, tk)
        kb = k_ref[0, pl.ds(start, width), :]
        s = _dot_nt(qs_ref[...], kb)
        smax = s[:, 0:LANES]
        for g in range(1, width // LANES):
            smax = jnp.maximum(smax, s[:, g * LANES:(g + 1) * LANES])
        s_ref[slot][:, 0:width] = s
        mx_ref[slot][...] = smax

    def begin(qi, width):
        q = q_ref[0, pl.ds(pl.multiple_of(qi * tq, tq), tq), :]
        qs_ref[0:tq, :] = jnp.where(lane < DA_HEAD_DIM, q, jnp.zeros_like(q))
        qs_ref[tq:rows, :] = jnp.where(lane >= DA_HEAD_DIM, q, jnp.zeros_like(q))
        look_ahead(0, 0, width)
        m_ref[...] = jnp.full(m_ref.shape, -jnp.inf, F32)
        l_ref[...] = jnp.zeros(l_ref.shape, F32)
        acc_ref[...] = jnp.zeros(acc_ref.shape, F32)

    def finish(qi):
        lam = (jnp.exp(jnp.sum(lq1_ref[...] * lk1_ref[...], axis=-1, keepdims=True))
               - jnp.exp(jnp.sum(lq2_ref[...] * lk2_ref[...], axis=-1, keepdims=True))
               + lambda_init)
        o = (acc_ref[0:tq, :] / l_ref[0:tq, :]
             - lam * (acc_ref[tq:rows, :] / l_ref[tq:rows, :]))
        ms = jnp.mean(o * o, axis=-1, keepdims=True)
        o = o * lax.rsqrt(ms + RMS_EPS) * g_ref[...] * (1.0 - lambda_init)
        o_ref[0, pl.ds(pl.multiple_of(qi * tq, tq), tq), :] = o.astype(BF16)

    def softmax_pv(jb, slot, width, mask_from, smax=None):
        ngrp = width // LANES
        start = pl.multiple_of(jb * tk, tk)
        vb = v_ref[0, pl.ds(start, width), :]
        row = lax.broadcasted_iota(jnp.int32, (rows, LANES), 0) & (tq - 1)
        col = lax.broadcasted_iota(jnp.int32, (rows, LANES), 1)

        def sgrp(g):
            x = s_ref[slot][:, g * LANES:(g + 1) * LANES]
            if g >= mask_from:
                x = jnp.where(col + (g - mask_from) * LANES <= row, x, -jnp.inf)
            return x

        if smax is None:
            smax = sgrp(0)
            for g in range(1, ngrp):
                smax = jnp.maximum(smax, sgrp(g))
        m_old = m_ref[...]
        m_new = jnp.maximum(m_old, jnp.max(smax, axis=-1, keepdims=True))
        m_ref[...] = m_new
        a = jnp.exp2(m_old - m_new)
        p = jnp.concatenate([jnp.exp2((sgrp(g) - m_new).astype(BF16)) for g in range(ngrp)],
                            axis=1)
        v1 = jnp.concatenate([vb, jnp.ones((width, LANES), BF16)], axis=1)
        pv = _dot(p, v1)
        acc_ref[...] = a * acc_ref[...] + pv[:, :DA_V_DIM]
        l_ref[...] = a * l_ref[...] + pv[:, DA_V_DIM:]

    def step(jb, slot, next_width=tk):
        look_ahead(jb + 1, 1 - slot, next_width)
        softmax_pv(jb, slot, tk, tk // LANES, smax=mx_ref[slot][...])

    def pair(p, carry):
        step(2 * p, 0)
        step(2 * p + 1, 1)
        return carry

    def query_block(i, carry):
        nfull = lax.shift_right_logical(i, 1)
        npair = lax.shift_right_logical(nfull, 1)
        even_i = (i & 1) == 0
        peel = ((nfull & 1) == 0) & (npair >= 1)
        lax.fori_loop(0, npair - peel.astype(jnp.int32), pair, 0)

        def last_pair(width):
            step(nfull - 2, 0)
            step(nfull - 1, 1, width)

        pl.when(peel & even_i)(functools.partial(last_pair, tq))
        pl.when(peel & jnp.logical_not(even_i))(functools.partial(last_pair, tk))

        def last_blocks(slot, width, mask_from):
            if slot == 1:
                step(nfull - 1, 0, width)
            softmax_pv(nfull, slot, width, mask_from)

        for slot in range(2):
            pl.when(even_i & ((nfull & 1) == slot))(functools.partial(last_blocks, slot, tq, 0))
            pl.when(jnp.logical_not(even_i) & ((nfull & 1) == slot))(
                functools.partial(last_blocks, slot, tk, tq // LANES))

        @pl.when(i + 1 < nq)
        def _():
            finish(i)
            begin(i + 1, tk)

        @pl.when(i + 1 == nq)
        def _():
            finish(i)

        return carry

    begin(0, tq)
    lax.fori_loop(0, nq, query_block, 0)


def _diff_attn(z3, lq1, lk1, lq2, lk2, subln_g, tq, tk, lambda_init):
    bsz, seq, _ = z3.shape
    assert tk == 2 * tq
    kern = functools.partial(_diff_attn_kernel, tq=tq, tk=tk, nq=seq // tq,
                             lambda_init=lambda_init)
    vec = lambda w: pl.BlockSpec((1, w), lambda b, h: (0, 0))
    return pl.pallas_call(
        kern,
        grid=(bsz, DA_HEADS),
        in_specs=[
            pl.BlockSpec((1, seq, LANES), lambda b, h: (b, 0, h)),
            pl.BlockSpec((1, seq, LANES), lambda b, h: (b, 0, DA_HEADS + h)),
            pl.BlockSpec((1, seq, LANES), lambda b, h: (b, 0, 2 * DA_HEADS + h)),
            vec(DA_HEAD_DIM), vec(DA_HEAD_DIM), vec(DA_HEAD_DIM), vec(DA_HEAD_DIM),
            vec(DA_V_DIM),
        ],
        out_specs=pl.BlockSpec((1, seq, LANES), lambda b, h: (b, 0, h)),
        out_shape=jax.ShapeDtypeStruct((bsz, seq, DA_HEADS * DA_V_DIM), BF16),
        scratch_shapes=[
            pltpu.VMEM((2 * tq, LANES), BF16),
            pltpu.VMEM((2 * tq, tk), F32),
            pltpu.VMEM((2 * tq, tk), F32),
            pltpu.VMEM((2 * tq, LANES), F32),
            pltpu.VMEM((2 * tq, LANES), F32),
            pltpu.VMEM((2 * tq, LANES), F32),
            pltpu.VMEM((2 * tq, LANES), F32),
            pltpu.VMEM((2 * tq, DA_V_DIM), F32),
        ],
        compiler_params=pltpu.CompilerParams(
            dimension_semantics=("arbitrary", "arbitrary"),
            vmem_limit_bytes=VMEM_LIMIT),
        name="diff_attn",
    )(z3, z3, z3, lq1, lk1, lq2, lk2, subln_g)


def _mix_kernel(u_ref, v_ref, xq_ref, gate_ref, oda_ref, x_ref, mem_ref, wkv_ref, ws_ref, bs_ref,
                wa_ref, wsg_ref, wm_ref, wo_ref, g1_ref, b1_ref, o_ref, osg_ref, oxa_ref, mkv_ref,
                *, tm, seq):
    @pl.when((pl.program_id(0) * tm) % seq == 0)
    def _():
        mkv_ref[...] = _dot(mem_ref[0].astype(BF16), wkv_ref[...]).astype(BF16)

    halves = [slice(h * MIX_HALF, (h + 1) * MIX_HALF) for h in range(tm // MIX_HALF)]
    d = D_MODEL
    row = lax.broadcasted_iota(jnp.int32, (SG_CHUNK, SG_CHUNK), 0)
    col = lax.broadcasted_iota(jnp.int32, (SG_CHUNK, SG_CHUNK), 1)
    ws = [jnp.where(col <= row, ws_ref[g], 0.0).astype(BF16) for g in range(SG_GROUPS)]

    def chunks(hs):
        return [slice(r, r + SG_CHUNK) for r in range(hs.start, hs.stop, SG_CHUNK)]

    def xa_cols(hd):
        return slice(hd * XA_HEAD_DIM, (hd + 1) * XA_HEAD_DIM)

    sg = {}
    for hs in halves:
        for g in range(SG_GROUPS):
            cs = slice(g * SG_GROUP_DIM, (g + 1) * SG_GROUP_DIM)
            for rs in chunks(hs):
                sg[rs.start, g] = _dot(ws[g], v_ref[rs, cs]) + bs_ref[:, g:g + 1]
    xs = [[_dot_nt(xq_ref[hs, xa_cols(hd)], mkv_ref[:, xa_cols(hd)]) * (XA_HEAD_DIM ** -0.5)
           for hd in range(XA_HEADS)] for hs in halves]
    t_attn = [_dot(oda_ref[hs, :], wa_ref[...]) for hs in halves]
    for hs in halves:
        for g in range(SG_GROUPS):
            cs = slice(g * SG_GROUP_DIM, (g + 1) * SG_GROUP_DIM)
            for rs in chunks(hs):
                osg_ref[rs, cs] = (u_ref[rs, cs].astype(F32) * sg[rs.start, g]).astype(BF16)
    t_sg = [_dot(osg_ref[hs, :], wsg_ref[...]) for hs in halves]
    for hi, hs in enumerate(halves):
        for hd in range(XA_HEADS):
            s = xs[hi][hd]
            p = jnp.exp(s - jnp.max(s, axis=-1, keepdims=True))
            p = p / jnp.sum(p, axis=-1, keepdims=True)
            vs = slice(XA_HEADS * XA_HEAD_DIM + hd * XA_HEAD_DIM,
                       XA_HEADS * XA_HEAD_DIM + (hd + 1) * XA_HEAD_DIM)
            oxa_ref[hs, xa_cols(hd)] = _dot(p.astype(BF16), mkv_ref[:, vs]).astype(BF16)
    t_mem = [_dot(oxa_ref[hs, :], wm_ref[...]) for hs in halves]
    merged = [(gate_ref[hs, 0:d].astype(F32) * t_attn[hi]
               + gate_ref[hs, d:2 * d].astype(F32) * t_sg[hi]
               + gate_ref[hs, 2 * d:3 * d].astype(F32) * t_mem[hi]).astype(BF16)
              for hi, hs in enumerate(halves)]
    y = [_dot(m, wo_ref[...]) for m in merged]
    for hi, hs in enumerate(halves):
        o_ref[hs, :] = _layer_norm(ALPHA * x_ref[hs, :] + y[hi], g1_ref[...], b1_ref[...])


def _mix(z2, oda2, x2, mem, wkv, w_s, b_s_t, wa, wsg, wm, wo, ln_g, ln_b, tm, seq):
    n = x2.shape[0]
    d = D_MODEL
    full = lambda shape: pl.BlockSpec(shape, lambda i: (0,) * len(shape),
                                      pipeline_mode=pl.Buffered(1))
    return pl.pallas_call(
        functools.partial(_mix_kernel, tm=tm, seq=seq),
        grid=(n // tm,),
        in_specs=[
            pl.BlockSpec((tm, d), lambda i: (i, 3)),
            pl.BlockSpec((tm, d), lambda i: (i, 4)),
            pl.BlockSpec((tm, d), lambda i: (i, 5)),
            pl.BlockSpec((tm, 3 * d), lambda i: (i, 2)),
            pl.BlockSpec((tm, d), lambda i: (i, 0)),
            pl.BlockSpec((tm, d), lambda i: (i, 0)),
            pl.BlockSpec((1,) + mem.shape[1:], lambda i: ((i * tm) // seq, 0, 0)),
            full(wkv.shape), full(w_s.shape), full(b_s_t.shape),
            full((d, d)), full((d, d)), full((d, d)), full((d, d)),
            full((1, d)), full((1, d)),
        ],
        out_specs=pl.BlockSpec((tm, d), lambda i: (i, 0)),
        out_shape=jax.ShapeDtypeStruct((n, d), F32),
        scratch_shapes=[pltpu.VMEM((tm, d), BF16), pltpu.VMEM((tm, d), BF16),
                        pltpu.VMEM((mem.shape[1], wkv.shape[1]), BF16)],
        compiler_params=pltpu.CompilerParams(
            dimension_semantics=("arbitrary",), vmem_limit_bytes=VMEM_LIMIT),
        name="mix",
    )(z2, z2, z2, z2, oda2, x2, mem, wkv, w_s, b_s_t, wa, wsg, wm, wo, ln_g, ln_b)


def _ffn_kernel(x_ref, wi_ref, wo_ref, g_ref, b_ref, o_ref, *, d_ff, tm):
    halves = [slice(h * MIX_HALF, (h + 1) * MIX_HALF) for h in range(tm // MIX_HALF)]
    hid = [_dot(x_ref[hs, :].astype(BF16), wi_ref[...]) for hs in halves]
    act = [((h[:, :d_ff] * _sigmoid(h[:, :d_ff])) * h[:, d_ff:]).astype(BF16) for h in hid]
    y = [_dot(a, wo_ref[...]) for a in act]
    for hi, hs in enumerate(halves):
        o_ref[hs, :] = _layer_norm(ALPHA * x_ref[hs, :] + y[hi], g_ref[...], b_ref[...])


def _ffn(x1, wi, wo, ln_g, ln_b, tm):
    n, d = x1.shape
    d_ff = wo.shape[0]
    full = lambda shape: pl.BlockSpec(shape, lambda i: (0,) * len(shape),
                                      pipeline_mode=pl.Buffered(1))
    return pl.pallas_call(
        functools.partial(_ffn_kernel, d_ff=d_ff, tm=tm),
        grid=(n // tm,),
        in_specs=[pl.BlockSpec((tm, d), lambda i: (i, 0)),
                  full(wi.shape), full(wo.shape), full((1, d)), full((1, d))],
        out_specs=pl.BlockSpec((tm, d), lambda i: (i, 0)),
        out_shape=jax.ShapeDtypeStruct((n, d), F32),
        compiler_params=pltpu.CompilerParams(
            dimension_semantics=("arbitrary",), vmem_limit_bytes=VMEM_LIMIT),
        name="ffn",
    )(x1, wi, wo, ln_g, ln_b)


def kernel(x, mem, positions, w_in, lambda_q1, lambda_k1, lambda_q2, lambda_k2, da_subln_g,
           sg_norm_g, sg_norm_b, sg_w_s, sg_b_s, w_mem_kv, w_br_attn, w_br_sg, w_br_mem, w_out,
           ln1_g, ln1_b, w_ffn_in, w_ffn_out, ln2_g, ln2_b):
    bsz, seq, d = x.shape
    n = bsz * seq
    depth = w_in.shape[0]
    assert d == D_MODEL and depth == DEPTH

    inv_freq = ROPE_THETA ** (-jnp.arange(ROT_HALF, dtype=F32) * 2.0 / ROT_DIM)
    invf = inv_freq.reshape(ROT_HALF, 1)
    pos2 = positions.reshape(n // PROJ_ROWS, 1, PROJ_ROWS)

    x2 = x.reshape(n, d)
    for l in range(depth):
        lambda_init = 0.8 - 0.6 * math.exp(-0.3 * l)
        row = lambda a: a[l].reshape(1, -1).astype(F32)
        z2 = _proj_in(x2, w_in[l].astype(BF16), pos2, invf, row(sg_norm_g), row(sg_norm_b),
                      tm=PROJ_ROWS)
        oda = _diff_attn(z2.reshape(bsz, seq, -1), row(lambda_q1), row(lambda_k1),
                         row(lambda_q2), row(lambda_k2), row(da_subln_g), tq=ATTN_Q_ROWS, tk=ATTN_K_ROWS,
                         lambda_init=lambda_init)
        x1 = _mix(z2, oda.reshape(n, -1), x2, mem, w_mem_kv[l].astype(BF16), sg_w_s[l], sg_b_s[l].T,
                  w_br_attn[l].astype(BF16), w_br_sg[l].astype(BF16), w_br_mem[l].astype(BF16),
                  w_out[l].astype(BF16), row(ln1_g), row(ln1_b), tm=MIX_ROWS, seq=seq)
        x2 = _ffn(x1, w_ffn_in[l].astype(BF16), w_ffn_out[l].astype(BF16), row(ln2_g), row(ln2_b),
                  tm=FFN_ROWS)
    return x2.reshape(bsz, seq, d)
```
